```python
import math
import numpy as np
import jax
import jax.numpy as jnp
from jax import lax

D_MODEL = 1024
BATCH = 8
SEQ = 4096
DEPTH = 1

GRID_W = 64
CTX_LEN = 256

RWKV_WIDTH = 512
RWKV_HEAD = 64
RWKV_HEADS = RWKV_WIDTH // RWKV_HEAD
DECAY_LORA = 32
AAA_LORA = 32
GATE_LORA = 96
DIFF_WIDTH = D_MODEL - RWKV_WIDTH
DIFF_HEAD = 64
DIFF_HEADS = DIFF_WIDTH // (2 * DIFF_HEAD)
D_FF = 2816
CONV_W = 3
Q_BLOCK = 128
ROPE_THETA = 10000.0
NORM_EPS = 1e-6
LN_X_EPS = 64e-5
SUBLN_EPS = 1e-5
ATTN_SCALE = DIFF_HEAD ** -0.5

RWKV_SPLITS = (RWKV_WIDTH, 2 * RWKV_WIDTH, 3 * RWKV_WIDTH,
               3 * RWKV_WIDTH + DECAY_LORA, 3 * RWKV_WIDTH + DECAY_LORA + AAA_LORA)
RWKV_COLS = 3 * RWKV_WIDTH + DECAY_LORA + AAA_LORA + GATE_LORA
DIFF_QK = DIFF_HEADS * 2 * DIFF_HEAD
DIFF_V = DIFF_HEADS * 2 * DIFF_HEAD
IN_COLS = RWKV_COLS + 2 * DIFF_QK + DIFF_V

kernel_name = 'hybrid_rwkv7_diffattn_dit_block'


def rmsnorm(x, g, eps=NORM_EPS):
    xf = x.astype(jnp.float32)
    y = xf * lax.rsqrt(jnp.mean(xf * xf, axis=-1, keepdims=True) + eps)
    return (y * g.astype(jnp.float32)).astype(x.dtype)


def modulate(h, shift, scale):
    return h * (1.0 + scale) + shift


def adaln(cond, w_mod, b_mod):
    return jax.nn.silu(cond) @ w_mod + b_mod


def dwconv(x, w):
    return lax.conv_general_dilated(
        x, w[:, None, :].astype(x.dtype), window_strides=(1,),
        padding=((CONV_W // 2, CONV_W // 2),),
        dimension_numbers=('NWC', 'WIO', 'NWC'),
        feature_group_count=x.shape[-1])


def axial_rope(rows):
    n_pair = DIFF_HEAD // 4
    row = jnp.repeat(jnp.arange(rows, dtype=jnp.float32), GRID_W)
    col = jnp.tile(jnp.arange(GRID_W, dtype=jnp.float32), rows)
    inv = ROPE_THETA ** (-jnp.arange(n_pair, dtype=jnp.float32) / n_pair)
    ang = jnp.concatenate([row[:, None] * inv, col[:, None] * inv], axis=-1)
    return jnp.cos(ang), jnp.sin(ang)


def apply_rope(x, cos, sin):
    half = DIFF_HEAD // 2
    x1, x2 = x[..., :half], x[..., half:]
    return jnp.concatenate([x1 * cos - x2 * sin, x1 * sin + x2 * cos], axis=-1).astype(x.dtype)


def rwkv_prepare(rw, p):
    B, T, _ = rw.shape
    heads = lambda t: t.reshape(B, T, RWKV_HEADS, RWKV_HEAD)
    r, k, v, wl, al, gl = jnp.split(rw, RWKV_SPLITS, axis=-1)
    kk = heads(k * p['k_k']).astype(jnp.float32)
    kk = kk / jnp.maximum(jnp.linalg.norm(kk, axis=-1, keepdims=True), 1e-12)
    w_lat = jnp.tanh(wl)
    dirs = []
    for w0, w2, a0, a2 in ((p['w0_fwd'], p['w2_fwd'], p['a0_fwd'], p['a2_fwd']),
                           (p['w0_bwd'], p['w2_bwd'], p['a0_bwd'], p['a2_bwd'])):
        w = -jax.nn.softplus(-(w0 + w_lat @ w2).astype(jnp.float32)) - 0.5
        decay = jnp.exp(-jnp.exp(w))
        a = jax.nn.sigmoid((a0 + al @ a2).astype(jnp.float32))
        k_dir = k * (1.0 + (a - 1.0) * p['k_a'])
        dirs.append((heads(decay), heads(k_dir), heads(a)))
    return heads(r), heads(v), kk, gl, dirs


def rwkv_scan(r, decay, k, v, kk, a, s0, reverse):
    tm = lambda t: jnp.swapaxes(t, 0, 1)

    def step(S, inp):
        r_t, w_t, k_t, v_t, kk_t, b_t = inp
        sa = jnp.einsum('bhvk,bhk->bhv', S, kk_t)
        S = (S * w_t[:, :, None, :] - sa[..., None] * b_t[:, :, None, :]
             + v_t[..., None] * k_t[:, :, None, :])
        return S, jnp.einsum('bhvk,bhk->bhv', S, r_t)

    s_fin, ys = lax.scan(step, s0, (tm(r), tm(decay), tm(k), tm(v), tm(kk), tm(kk * a)),
                         reverse=reverse)
    return tm(ys), s_fin


def rwkv_output(y, r, v, k_bonus, gl, p):
    B, T = y.shape[:2]
    mu = jnp.mean(y, axis=-1, keepdims=True)
    var = jnp.mean(jnp.square(y - mu), axis=-1, keepdims=True)
    yn = ((y - mu) * lax.rsqrt(var + LN_X_EPS)).reshape(B, T, RWKV_WIDTH) * p['ln_x_w'] + p['ln_x_b']
    bonus = (jnp.sum(r * k_bonus * p['r_k'], axis=-1, keepdims=True) * v).reshape(B, T, RWKV_WIDTH)
    g = jax.nn.sigmoid(gl) @ p['g2']
    return ((yn + bonus) * g).astype(r.dtype)


def rwkv_mixer(rw, rw_c, p, need_ctx):
    r, v, kk, gl, dirs = rwkv_prepare(rw, p)
    r_c, v_c, kk_c, gl_c, dirs_c = rwkv_prepare(rw_c, p)
    s0 = jnp.zeros((rw.shape[0], RWKV_HEADS, RWKV_HEAD, RWKV_HEAD), jnp.float32)
    ys, ys_c = [], []
    for d, reverse in enumerate((False, True)):
        dec_c, k_c, a_c = dirs_c[d]
        yd_c, s_ctx = rwkv_scan(r_c, dec_c, k_c, v_c, kk_c, a_c, s0, reverse)
        dec, k_d, a_d = dirs[d]
        yd, _ = rwkv_scan(r, dec, k_d, v, kk, a_d, s_ctx, reverse)
        ys.append(yd)
        ys_c.append(yd_c)
    out = rwkv_output(ys[0] + ys[1], r, v, 0.5 * (dirs[0][1] + dirs[1][1]), gl, p)
    out_c = (rwkv_output(ys_c[0] + ys_c[1], r_c, v_c, 0.5 * (dirs_c[0][1] + dirs_c[1][1]), gl_c, p)
             if need_ctx else None)
    return out, out_c


def diff_split(dq):
    B, T, _ = dq.shape
    q, k, v = jnp.split(dq, (DIFF_QK, 2 * DIFF_QK), axis=-1)
    qk = lambda t: t.reshape(B, T, DIFF_HEADS, 2, DIFF_HEAD).transpose(0, 2, 3, 1, 4)
    return qk(q), qk(k), v.reshape(B, T, DIFF_HEADS, 2 * DIFF_HEAD).transpose(0, 2, 1, 3)


def diff_attention(q, k_all, v_all, lam):
    s = jnp.einsum('bhmqd,bhmkd->bhmqk', q, k_all).astype(jnp.float32) * ATTN_SCALE
    pr = jax.nn.softmax(s, axis=-1)
    a = pr[:, :, 0] - lam * pr[:, :, 1]
    return jnp.einsum('bhqk,bhke->bhqe', a.astype(v_all.dtype), v_all)


def diff_post(o, p, lam_init):
    B, H, T, e = o.shape
    o = rmsnorm(o, p['subln_w'], SUBLN_EPS) * (1.0 - lam_init)
    return o.transpose(0, 2, 1, 3).reshape(B, T, H * e)


def diff_mixer(dq, dq_c, p, lam_init, cos, sin, need_ctx):
    q, k, v = diff_split(dq)
    q_c, k_c, v_c = diff_split(dq_c)
    q, k = apply_rope(q, cos, sin), apply_rope(k, cos, sin)
    f32 = lambda t: t.astype(jnp.float32)
    lam = (jnp.exp(jnp.sum(f32(p['lam_q1']) * f32(p['lam_k1'])))
           - jnp.exp(jnp.sum(f32(p['lam_q2']) * f32(p['lam_k2']))) + lam_init)
    k_all = jnp.concatenate([k_c, k], axis=3)
    v_all = jnp.concatenate([v_c, v], axis=2)
    B, H, _, T, d = q.shape
    nb = T // Q_BLOCK
    qb = q.reshape(B, H, 2, nb, Q_BLOCK, d).transpose(3, 0, 1, 2, 4, 5)
    o = lax.map(lambda qq: diff_attention(qq, k_all, v_all, lam), qb)
    o = o.transpose(1, 2, 0, 3, 4).reshape(B, H, T, 2 * d)
    out = diff_post(o, p, lam_init)
    out_c = diff_post(diff_attention(q_c, k_c, v_c, lam), p, lam_init) if need_ctx else None
    return out, out_c


def token_mixer(h, h_c, p, lam_init, cos, sin, need_ctx):
    parts = h @ p['w_in']
    parts_c = h_c @ p['w_in']
    rw = dwconv(parts[..., :RWKV_COLS], p['rwkv_conv'])
    rw_c = dwconv(parts_c[..., :RWKV_COLS], p['rwkv_conv'])
    y_r, y_r_c = rwkv_mixer(rw, rw_c, p, need_ctx)
    y_d, y_d_c = diff_mixer(parts[..., RWKV_COLS:], parts_c[..., RWKV_COLS:], p, lam_init,
                            cos, sin, need_ctx)
    out = jnp.concatenate([y_r, y_d], axis=-1) @ p['w_out']
    out_c = jnp.concatenate([y_r_c, y_d_c], axis=-1) @ p['w_out'] if need_ctx else None
    return out, out_c


def conv_ffn(h, p):
    u = dwconv(h @ p['w_up'], p['ffn_conv']) + p['ffn_conv_b']
    val, gate = jnp.split(u, 2, axis=-1)
    return (val * jax.nn.silu(gate)) @ p['w_down']


def setup_inputs(seed: int = 0) -> dict:
    key = jax.random.key(seed)
    keys = jax.random.split(key, 36)
    L, D, Rw = DEPTH, D_MODEL, RWKV_WIDTH

    def nrm(i, shape, s):
        return s * jax.random.normal(keys[i], shape, jnp.float32)

    def uni(i, shape, lo, hi):
        return jax.random.uniform(keys[i], shape, jnp.float32, lo, hi)

    conv_base = jnp.array([0.25, 0.5, 0.25], jnp.float32)[None, :, None]
    return {
        'x': nrm(0, (BATCH, SEQ, D), 1.0),
        'c': nrm(1, (BATCH, D), 1.0),
        'ctx': nrm(2, (BATCH, CTX_LEN, D), 1.0),
        'c_ctx': nrm(3, (D,), 1.0),
        'w_mod': nrm(4, (L, D, 6 * D), 0.5 * D ** -0.5),
        'b_mod': nrm(5, (L, 6 * D), 0.02),
        'g_pre_mix': 1.0 + nrm(6, (L, D), 0.05),
        'g_post_mix': 1.0 + nrm(7, (L, D), 0.05),
        'g_pre_ffn': 1.0 + nrm(8, (L, D), 0.05),
        'g_post_ffn': 1.0 + nrm(9, (L, D), 0.05),
        'w_in': nrm(10, (L, D, IN_COLS), D ** -0.5),
        'rwkv_conv': conv_base + nrm(11, (L, CONV_W, RWKV_COLS), 0.1),
        'w0_fwd': uni(12, (L, Rw), -6.5, -1.5),
        'w2_fwd': nrm(13, (L, DECAY_LORA, Rw), 0.1),
        'a0_fwd': nrm(14, (L, Rw), 0.1),
        'a2_fwd': nrm(15, (L, AAA_LORA, Rw), 0.1),
        'w0_bwd': uni(16, (L, Rw), -6.5, -1.5),
        'w2_bwd': nrm(17, (L, DECAY_LORA, Rw), 0.1),
        'a0_bwd': nrm(18, (L, Rw), 0.1),
        'a2_bwd': nrm(19, (L, AAA_LORA, Rw), 0.1),
        'g2': nrm(20, (L, GATE_LORA, Rw), GATE_LORA ** -0.5),
        'k_k': 0.85 + nrm(21, (L, Rw), 0.05),
        'k_a': 1.0 + nrm(22, (L, Rw), 0.05),
        'r_k': nrm(23, (L, RWKV_HEADS, RWKV_HEAD), 0.1),
        'ln_x_w': 1.0 + nrm(24, (L, Rw), 0.05),
        'ln_x_b': nrm(25, (L, Rw), 0.02),
        'lam_q1': nrm(26, (L, DIFF_HEAD), 0.1),
        'lam_k1': nrm(27, (L, DIFF_HEAD), 0.1),
        'lam_q2': nrm(28, (L, DIFF_HEAD), 0.1),
        'lam_k2': nrm(29, (L, DIFF_HEAD), 0.1),
        'subln_w': 1.0 + nrm(30, (L, 2 * DIFF_HEAD), 0.05),
        'w_out': nrm(31, (L, D, D), D ** -0.5),
        'w_up': nrm(32, (L, D, 2 * D_FF), D ** -0.5),
        'ffn_conv': conv_base + nrm(33, (L, CONV_W, 2 * D_FF), 0.1),
        'ffn_conv_b': nrm(34, (L, 2 * D_FF), 0.02),
        'w_down': nrm(35, (L, D_FF, D), D_FF ** -0.5),
    }


def reference(x, c, ctx, c_ctx, w_mod, b_mod, g_pre_mix, g_post_mix, g_pre_ffn, g_post_ffn,
              w_in, rwkv_conv, w0_fwd, w2_fwd, a0_fwd, a2_fwd, w0_bwd, w2_bwd, a0_bwd, a2_bwd,
              g2, k_k, k_a, r_k, ln_x_w, ln_x_b, lam_q1, lam_k1, lam_q2, lam_k2, subln_w,
              w_out, w_up, ffn_conv, ffn_conv_b, w_down):
    n_tok = x.shape[1]
    ROWS = n_tok // GRID_W
    cos, sin = axial_rope(ROWS)
    for l in range(DEPTH):
        p = dict(g_pre_mix=g_pre_mix[l], g_post_mix=g_post_mix[l], g_pre_ffn=g_pre_ffn[l],
                 g_post_ffn=g_post_ffn[l], w_in=w_in[l], rwkv_conv=rwkv_conv[l],
                 w0_fwd=w0_fwd[l], w2_fwd=w2_fwd[l], a0_fwd=a0_fwd[l], a2_fwd=a2_fwd[l],
                 w0_bwd=w0_bwd[l], w2_bwd=w2_bwd[l], a0_bwd=a0_bwd[l], a2_bwd=a2_bwd[l],
                 g2=g2[l], k_k=k_k[l], k_a=k_a[l], r_k=r_k[l], ln_x_w=ln_x_w[l], ln_x_b=ln_x_b[l],
                 lam_q1=lam_q1[l], lam_k1=lam_k1[l], lam_q2=lam_q2[l], lam_k2=lam_k2[l],
                 subln_w=subln_w[l], w_out=w_out[l], w_up=w_up[l], ffn_conv=ffn_conv[l],
                 ffn_conv_b=ffn_conv_b[l], w_down=w_down[l])
        lam_init = 0.8 - 0.6 * math.exp(-0.3 * l)
        need_ctx = l + 1 < DEPTH
        mod = adaln(c, w_mod[l], b_mod[l])[:, None, :]
        mod_c = adaln(c_ctx[None, :], w_mod[l], b_mod[l])[:, None, :]
        sh1, sc1, gt1, sh2, sc2, gt2 = jnp.split(mod, 6, axis=-1)
        sh1c, sc1c, gt1c, sh2c, sc2c, gt2c = jnp.split(mod_c, 6, axis=-1)
        h = modulate(rmsnorm(x, p['g_pre_mix']), sh1, sc1)
        h_c = modulate(rmsnorm(ctx, p['g_pre_mix']), sh1c, sc1c)
        y, y_c = token_mixer(h, h_c, p, lam_init, cos, sin, need_ctx)
        x = x + gt1 * rmsnorm(y, p['g_post_mix'])
        h = modulate(rmsnorm(x, p['g_pre_ffn']), sh2, sc2)
        x = x + gt2 * rmsnorm(conv_ffn(h, p), p['g_post_ffn'])
        if need_ctx:
            ctx = ctx + gt1c * rmsnorm(y_c, p['g_post_mix'])
            h_c = modulate(rmsnorm(ctx, p['g_pre_ffn']), sh2c, sc2c)
            ctx = ctx + gt2c * rmsnorm(conv_ffn(h_c, p), p['g_post_ffn'])
    return x
```

```python
import functools
import math

import jax
import jax.numpy as jnp
from jax import lax
from jax.experimental import pallas as pl
from jax.experimental.pallas import tpu as pltpu

F32 = jnp.float32
BF16 = jnp.bfloat16

LANE = 128
SUBLANE = 8
TOK_TILE = 256
CHUNK = 64
FFN_TILE = 512
FFN_COLS = 256
VMEM_LIMIT = 56 * 1024 * 1024

GRID_W = 64
ROPE_THETA = 10000.0
NORM_EPS = 1e-6
LN_X_EPS = 64e-5
SUBLN_EPS = 1e-5

_NT = (((1,), (1,)), ((), ()))
_TN = (((0,), (0,)), ((), ()))


def _dot(a, b):
    return jnp.dot(a.astype(BF16), b.astype(BF16), preferred_element_type=F32)


def _dot_nt(a, b):
    return lax.dot_general(a.astype(BF16), b.astype(BF16), _NT, preferred_element_type=F32)


def _dot_tn(a, b):
    return lax.dot_general(a.astype(BF16), b.astype(BF16), _TN, preferred_element_type=F32)


def _split(a):
    hi = a.astype(BF16)
    lo = (a - hi.astype(F32)).astype(BF16)
    return hi, lo


def _dot_hl(a, b_exact):
    hi, lo = _split(a)
    return (jnp.dot(hi, b_exact, preferred_element_type=F32)
            + jnp.dot(lo, b_exact, preferred_element_type=F32))


def _dot_lh(a_exact, b):
    hi, lo = _split(b)
    return (jnp.dot(a_exact, hi, preferred_element_type=F32)
            + jnp.dot(a_exact, lo, preferred_element_type=F32))


def _dot3(a, b):
    ah, al = _split(a)
    bh, bl = _split(b)
    return (jnp.dot(ah, bh, preferred_element_type=F32)
            + jnp.dot(ah, bl, preferred_element_type=F32)
            + jnp.dot(al, bh, preferred_element_type=F32))


_mm_gram = _dot_nt
_mm_inv = _dot
_mm_apply = _dot
_mm_tn = _dot_tn
_mm_state = _dot


def _sigmoid(z):
    return 1.0 / (1.0 + jnp.exp(-z))


def _rms(x, g, eps):
    return x * lax.rsqrt(jnp.mean(x * x, axis=-1, keepdims=True) + eps) * g


def _group_ones(width, group):
    i = lax.broadcasted_iota(jnp.int32, (width, width), 0) // group
    j = lax.broadcasted_iota(jnp.int32, (width, width), 1) // group
    return jnp.where(i == j, 1.0, 0.0).astype(BF16)


def _shift_rows(x, prev_row, next_row):
    n = x.shape[0]
    row = lax.broadcasted_iota(jnp.int32, (n, 1), 0)
    xm1 = jnp.where(row == 0, prev_row, pltpu.roll(x, 1, axis=0))
    xp1 = jnp.where(row == n - 1, next_row, pltpu.roll(x, n - 1, axis=0))
    return xm1, xp1


def _params(*sem):
    return pltpu.CompilerParams(dimension_semantics=sem, vmem_limit_bytes=VMEM_LIMIT)


def _const_spec(shape):
    n = len(shape)
    return pl.BlockSpec(shape, lambda *_: (0,) * n, pipeline_mode=pl.Buffered(1))


def _mod_kernel(c_ref, w_ref, b_ref, o_ref):
    c = c_ref[...]
    o_ref[...] = _dot3(c * _sigmoid(c), w_ref[...]) + b_ref[...]


def _modulation(cc, w_mod, b_mod):
    rows, d = cc.shape
    n = w_mod.shape[1]
    tn = 1024
    return pl.pallas_call(
        _mod_kernel,
        grid=(n // tn,),
        in_specs=[pl.BlockSpec((rows, d), lambda j: (0, 0)),
                  pl.BlockSpec((d, tn), lambda j: (0, j)),
                  pl.BlockSpec((1, tn), lambda j: (0, j))],
        out_specs=pl.BlockSpec((rows, tn), lambda j: (0, j)),
        out_shape=jax.ShapeDtypeStruct((rows, n), F32),
        compiler_params=_params("parallel"),
        name="adaln_mod",
    )(cc, w_mod, b_mod)


def _inproj_kernel(x_ref, ctx_ref, mod_ref, g_ref, w_ref, rc_ref, rs1_ref, rs2_ref,
                   rw_ref, q_ref, k_ref, v_ref, *, n_rw, width, scale):
    t = pl.program_id(1)
    xin = jnp.where(t == 0, ctx_ref[0], x_ref[0])
    mod = mod_ref[0, 0]
    h = (_rms(xin, g_ref[...], NORM_EPS) * (1.0 + mod[1:2]) + mod[0:1]).astype(BF16)

    step = 4 * LANE
    for c0 in range(0, n_rw, step):
        c1 = min(c0 + step, n_rw)
        rw_ref[0, :, c0:c1] = jnp.dot(h, w_ref[:, c0:c1], preferred_element_type=F32)

    reps = width // LANE
    cos = jnp.concatenate([rc_ref[...]] * reps, axis=1)
    s1 = jnp.concatenate([rs1_ref[...]] * reps, axis=1)
    s2 = jnp.concatenate([rs2_ref[...]] * reps, axis=1)
    half = GRID_W // 2

    def rope(z):
        return z * cos + pltpu.roll(z, width - half, axis=1) * s1 + pltpu.roll(z, half, axis=1) * s2

    q = jnp.dot(h, w_ref[:, n_rw:n_rw + width], preferred_element_type=F32)
    q_ref[0] = (rope(q) * scale).astype(BF16)
    k = jnp.dot(h, w_ref[:, n_rw + width:n_rw + 2 * width], preferred_element_type=F32)
    k_ref[0] = rope(k).astype(BF16)
    v = jnp.dot(h, w_ref[:, n_rw + 2 * width:n_rw + 3 * width], preferred_element_type=F32)
    v_ref[0] = v.astype(BF16)


def _inproj(x, ctx, mod1, g_pre, w_p, rope_c, rope_s1, rope_s2, n_rw, width, scale):
    b, t_lat, d = x.shape
    n_tiles = (t_lat + ctx.shape[1]) // TOK_TILE
    t_all = n_tiles * TOK_TILE
    tok = lambda w: pl.BlockSpec((1, TOK_TILE, w), lambda i, t: (i, t, 0))
    return pl.pallas_call(
        functools.partial(_inproj_kernel, n_rw=n_rw, width=width, scale=scale),
        grid=(b, n_tiles),
        in_specs=[pl.BlockSpec((1, TOK_TILE, d), lambda i, t: (i, jnp.maximum(t - 1, 0), 0)),
                  pl.BlockSpec((1, TOK_TILE, d), lambda i, t: (i, 0, 0)),
                  pl.BlockSpec((1, 1, 2, d), lambda i, t: (i, jnp.minimum(t, 1), 0, 0)),
                  _const_spec((1, d)),
                  _const_spec(w_p.shape),
                  pl.BlockSpec((TOK_TILE, LANE), lambda i, t: (t, 0)),
                  pl.BlockSpec((TOK_TILE, LANE), lambda i, t: (t, 0)),
                  pl.BlockSpec((TOK_TILE, LANE), lambda i, t: (t, 0))],
        out_specs=[tok(n_rw), tok(width), tok(width), tok(width)],
        out_shape=[jax.ShapeDtypeStruct((b, t_all, n_rw), F32),
                   jax.ShapeDtypeStruct((b, t_all, width), BF16),
                   jax.ShapeDtypeStruct((b, t_all, width), BF16),
                   jax.ShapeDtypeStruct((b, t_all, width), BF16)],
        compiler_params=_params("parallel", "arbitrary"),
        name="in_proj",
    )(x, ctx, mod1, g_pre, w_p, rope_c, rope_s1, rope_s2)


def _attn_kernel(q_ref, k_ref, v_ref, lam_ref, g_ref, o_ref, *, lam_init):
    lp = lam_ref[...]
    lam = (jnp.exp(jnp.sum(lp[0:1] * lp[1:2], axis=-1, keepdims=True))
           - jnp.exp(jnp.sum(lp[2:3] * lp[3:4], axis=-1, keepdims=True)) + lam_init)
    q = q_ref[0]
    k = k_ref[0]
    d = q.shape[-1] // 2
    lane = lax.broadcasted_iota(jnp.int32, q.shape, 1)
    zero = jnp.zeros_like(q)

    def probs(qm):
        s = lax.dot_general(qm, k, _NT, preferred_element_type=F32)
        p = jnp.exp(s - jnp.max(s, axis=-1, keepdims=True))
        return p, 1.0 / jnp.sum(p, axis=-1, keepdims=True)

    p1, r1 = probs(jnp.where(lane < d, q, zero))
    p2, r2 = probs(jnp.where(lane >= d, q, zero))
    a = (p1 * r1 - p2 * (lam * r2)).astype(BF16)
    o = jnp.dot(a, v_ref[0], preferred_element_type=F32)
    o_ref[0] = (_rms(o, g_ref[...], SUBLN_EPS) * (1.0 - lam_init)).astype(BF16)


def _attention(q, k, v, lam_p, subln_w, t_lat, lam_init):
    b, t_all, width = q.shape
    hd = subln_w.shape[-1]
    heads = width // hd
    off = (t_all - t_lat) // TOK_TILE
    return pl.pallas_call(
        functools.partial(_attn_kernel, lam_init=lam_init),
        grid=(b, heads, t_lat // TOK_TILE),
        in_specs=[pl.BlockSpec((1, TOK_TILE, hd), lambda i, h, j: (i, j + off, h)),
                  pl.BlockSpec((1, t_all, hd), lambda i, h, j: (i, 0, h)),
                  pl.BlockSpec((1, t_all, hd), lambda i, h, j: (i, 0, h)),
                  _const_spec(lam_p.shape),
                  _const_spec((1, hd))],
        out_specs=pl.BlockSpec((1, TOK_TILE, hd), lambda i, h, j: (i, j, h)),
        out_shape=jax.ShapeDtypeStruct((b, t_lat, width), BF16),
        compiler_params=_params("parallel", "parallel", "arbitrary"),
        name="diff_attn",
    )(q, k, v, lam_p, subln_w)


def _scan_kernel(main_ref, prev_ref, next_ref, cw_ref, kk_ref, ka_ref, rk_ref,
                 w0_ref, w2_ref, a0_ref, a2_ref,
                 y_ref, bon_ref,
                 h_s, rh_s, kh_s, kt_s, bt_s, kb_s, bb_s, v_s, gam_s,
                 p_s, q_s, r_s, yy_s, yo_s, *, n_tiles, rw, heads):
    d = pl.program_id(1)
    s = pl.program_id(2)
    fwd = d == 0
    tile = jnp.where(fwd, s, jnp.where(s == 0, 0, n_tiles - s))
    hd = rw // heads
    n_chunks = TOK_TILE // CHUNK

    @pl.when(s == 0)
    def _():
        h_s[...] = jnp.zeros_like(h_s)

    x = main_ref[0]
    starts = (tile == 0) | (tile == 1)
    ends = (tile == 0) | (tile == n_tiles - 1)
    prev_row = jnp.where(starts, 0.0, prev_ref[0][SUBLANE - 1:SUBLANE, :])
    next_row = jnp.where(ends, 0.0, next_ref[0][0:1, :])
    xm1, xp1 = _shift_rows(x, prev_row, next_row)
    cw = cw_ref[...]
    xc = cw[0:1] * xm1 + cw[1:2] * x + cw[2:3] * xp1

    r = xc[:, 0:rw]
    k = xc[:, rw:2 * rw]
    v = xc[:, 2 * rw:3 * rw]
    wl = xc[:, 3 * rw:3 * rw + LANE]
    al = xc[:, 3 * rw + LANE:3 * rw + 2 * LANE]

    g_ones = _group_ones(rw, hd)
    kk = k * kk_ref[...]
    kk = kk / jnp.maximum(jnp.sqrt(_dot_hl(kk * kk, g_ones)), 1e-12)
    z = w0_ref[0] + _dot(jnp.tanh(wl), w2_ref[0])
    w = -(jnp.maximum(-z, 0.0) + jnp.log(1.0 + jnp.exp(-jnp.abs(z)))) - 0.5
    ld = -jnp.exp(w)
    a = _sigmoid(a0_ref[0] + _dot(al, a2_ref[0]))
    kd = k * (1.0 + (a - 1.0) * ka_ref[...])
    bvec = kk * a
    bon_ref[0, 0] = 0.5 * _dot_hl(r * kd * rk_ref[...], g_ones) * v

    ri = lax.broadcasted_iota(jnp.int32, (TOK_TILE, TOK_TILE), 0)
    ci = lax.broadcasted_iota(jnp.int32, (TOK_TILE, TOK_TILE), 1)
    same = (ri // CHUNK) == (ci // CHUNK)
    sign = jnp.where(fwd, 1, -1)
    before = (ri - ci) * sign >= 0
    tri = jnp.where(same & before, 1.0, 0.0).astype(BF16)
    tot = jnp.where(same, 1.0, 0.0).astype(BF16)
    cum = _dot_lh(tri, ld)
    ctot = _dot_lh(tot, ld)
    e_neg = jnp.exp(-cum)
    e_rem = jnp.exp(ctot - cum)
    full = {"rh": r * jnp.exp(cum), "kh": kk * jnp.exp(cum - ld), "kt": kd * e_neg,
            "bt": bvec * e_neg, "kb": kd * e_rem, "bb": bvec * e_rem, "v": v}
    refs = {"rh": rh_s, "kh": kh_s, "kt": kt_s, "bt": bt_s, "kb": kb_s, "bb": bb_s, "v": v_s}
    gam = jnp.exp(ctot)
    for h in range(heads):
        sl = slice(h * hd, (h + 1) * hd)
        for name, val in full.items():
            refs[name][h] = val[:, sl].astype(BF16)
        gam_s[h] = gam[:, sl]

    ii = lax.broadcasted_iota(jnp.int32, (CHUNK, CHUNK), 0)
    jj = lax.broadcasted_iota(jnp.int32, (CHUNK, CHUNK), 1)
    strict = (ii - jj) * sign > 0
    incl = (ii - jj) * sign >= 0
    eye = ii == jj
    eye_f = jnp.where(eye, 1.0, 0.0)
    level_masks = []
    size = 1
    while size < CHUNK:
        level_masks.append(((ii // (2 * size)) == (jj // (2 * size))) & ((ii // size) != (jj // size)))
        size *= 2

    def build(idx, carry):
        h = idx // n_chunks
        c = idx % n_chunks
        rows = pl.ds(pl.multiple_of(c * CHUNK, CHUNK), CHUNK)
        rh = rh_s[h, rows, :]
        kh = kh_s[h, rows, :]
        kt = kt_s[h, rows, :]
        bt = bt_s[h, rows, :]
        kb = kb_s[h, rows, :]
        bb = bb_s[h, rows, :]
        vv = v_s[h, rows, :]
        g_row = gam_s[h, pl.ds(pl.multiple_of(c * CHUNK, CHUNK), SUBLANE), :][0:1]
        m_ab = jnp.where(strict, _mm_gram(kh, bt), 0.0)
        m_ak = jnp.where(strict, _mm_gram(kh, kt), 0.0)
        a_rk = jnp.where(incl, _mm_gram(rh, kt), 0.0)
        a_rb = jnp.where(incl, _mm_gram(rh, bt), 0.0)
        tinv = eye_f - jnp.where(level_masks[0], m_ab, 0.0)
        for lm in level_masks[1:]:
            tinv = tinv - _mm_inv(tinv, _mm_inv(jnp.where(lm, m_ab, 0.0), tinv))
        wmat = _mm_apply(tinv, kh)
        umat = _mm_apply(tinv, _mm_apply(m_ak, vv))
        p_s[idx] = jnp.where(eye, g_row, 0.0) - _mm_tn(bb, wmat)
        q_s[idx] = _mm_tn(kb, vv) - _mm_tn(bb, umat)
        r_s[idx] = rh.astype(F32) - _mm_apply(a_rb, wmat)
        yy_s[idx] = _mm_apply(a_rk, vv) - _mm_apply(a_rb, umat)
        return carry

    lax.fori_loop(0, heads * n_chunks, build, 0)

    for j in range(n_chunks):
        c = jnp.where(fwd, j, n_chunks - 1 - j)
        rows = pl.ds(pl.multiple_of(c * CHUNK, CHUNK), CHUNK)
        for h in range(heads):
            idx = h * n_chunks + c
            hb = h_s[h]
            yo_s[h, rows, :] = _mm_state(r_s[idx], hb) + yy_s[idx]
            h_s[h] = _mm_state(p_s[idx], hb) + q_s[idx]

    for h in range(heads):
        y_ref[0, 0, :, h * hd:(h + 1) * hd] = yo_s[h]


def _rwkv_scan(rwp, cw, k_k, k_a, r_k, w0, w2p, a0, a2p, t_lat, rw, heads):
    b, t_all, n_rw = rwp.shape
    n_tiles = t_all // TOK_TILE
    n_lat = t_lat // TOK_TILE
    hd = rw // heads
    halo = TOK_TILE // SUBLANE
    n_halo = t_all // SUBLANE

    def tile_of(d, s):
        return jnp.where(d == 0, s, jnp.where(s == 0, 0, n_tiles - s))

    def out_block(d, s):
        return jnp.clip(tile_of(d, s) - 1 + jnp.where((d == 1) & (s == 0), n_lat, 0), 0, n_lat - 1)

    out_spec = pl.BlockSpec((1, 1, TOK_TILE, rw), lambda i, d, s: (d, i, out_block(d, s), 0))
    dir_vec = pl.BlockSpec((1, 1, rw), lambda i, d, s: (d, 0, 0))
    dir_mat = pl.BlockSpec((1, LANE, rw), lambda i, d, s: (d, 0, 0))
    n_idx = heads * (TOK_TILE // CHUNK)
    head_bf = pltpu.VMEM((heads, TOK_TILE, hd), BF16)
    head_f32 = pltpu.VMEM((heads, TOK_TILE, hd), F32)
    mat = pltpu.VMEM((n_idx, CHUNK, hd), F32)
    return pl.pallas_call(
        functools.partial(_scan_kernel, n_tiles=n_tiles, rw=rw, heads=heads),
        grid=(b, 2, n_tiles),
        in_specs=[pl.BlockSpec((1, TOK_TILE, n_rw), lambda i, d, s: (i, tile_of(d, s), 0)),
                  pl.BlockSpec((1, SUBLANE, n_rw),
                               lambda i, d, s: (i, jnp.maximum(tile_of(d, s) * halo - 1, 0), 0)),
                  pl.BlockSpec((1, SUBLANE, n_rw),
                               lambda i, d, s: (i, jnp.minimum((tile_of(d, s) + 1) * halo, n_halo - 1), 0)),
                  _const_spec(cw.shape), _const_spec((1, rw)), _const_spec((1, rw)), _const_spec((1, rw)),
                  dir_vec, dir_mat, dir_vec, dir_mat],
        out_specs=[out_spec, out_spec],
        out_shape=[jax.ShapeDtypeStruct((2, b, t_lat, rw), F32),
                   jax.ShapeDtypeStruct((2, b, t_lat, rw), F32)],
        scratch_shapes=[pltpu.VMEM((heads, hd, hd), F32),
                        head_bf, head_bf, head_bf, head_bf, head_bf, head_bf, head_bf, head_f32,
                        mat, mat, mat, mat, head_f32],
        compiler_params=_params("parallel", "arbitrary", "arbitrary"),
        name="rwkv_scan",
    )(rwp, rwp, rwp, cw, k_k, k_a, r_k, w0, w2p, a0, a2p)


def _mix_kernel(yf_ref, yb_ref, bf_ref, bb_ref, gl_ref, glp_ref, gln_ref, cwg_ref, g2_ref,
                lnw_ref, lnb_ref, yd_ref, wo_ref, gpost_ref, x_ref, gt_ref, gpre_ref, mod_ref,
                x1_ref, h2_ref, *, n_lat, rw, heads):
    i = pl.program_id(1)
    hd = rw // heads
    g_ones = _group_ones(rw, hd)
    y = yf_ref[0, 0] + yb_ref[0, 0]
    mu = _dot_hl(y, g_ones) * (1.0 / hd)
    yc = y - mu
    var = _dot_hl(yc * yc, g_ones) * (1.0 / hd)
    yn = yc * lax.rsqrt(var + LN_X_EPS) * lnw_ref[...] + lnb_ref[...]

    gl = gl_ref[0]
    prev_row = jnp.where(i == 0, 0.0, glp_ref[0][SUBLANE - 1:SUBLANE, :])
    next_row = jnp.where(i == n_lat - 1, 0.0, gln_ref[0][0:1, :])
    gm1, gp1 = _shift_rows(gl, prev_row, next_row)
    cw = cwg_ref[...]
    glc = cw[0:1] * gm1 + cw[1:2] * gl + cw[2:3] * gp1
    gate = _dot(_sigmoid(glc), g2_ref[...])

    yr = ((yn + bf_ref[0, 0] + bb_ref[0, 0]) * gate).astype(BF16)
    m = (jnp.dot(yr, wo_ref[0:rw, :], preferred_element_type=F32)
         + jnp.dot(yd_ref[0], wo_ref[rw:, :], preferred_element_type=F32))
    x1 = x_ref[0] + gt_ref[0] * _rms(m, gpost_ref[...], NORM_EPS)
    x1_ref[0] = x1
    mod = mod_ref[0]
    h2_ref[0] = (_rms(x1, gpre_ref[...], NORM_EPS) * (1.0 + mod[1:2]) + mod[0:1]).astype(BF16)


def _mix_out(y, bon, rwp, cwg, g2p, ln_w, ln_b, yd, w_out, g_post, x, gt1, g_pre2, mod2, rw, heads):
    b, t_lat, d = x.shape
    n_lat = t_lat // TOK_TILE
    off = (rwp.shape[1] - t_lat) // TOK_TILE
    gl_blk = (3 * rw + 2 * LANE) // LANE
    halo = TOK_TILE // SUBLANE
    dir_spec = lambda dd: pl.BlockSpec((1, 1, TOK_TILE, rw), lambda i, j: (dd, i, j, 0))
    tok = lambda w: pl.BlockSpec((1, TOK_TILE, w), lambda i, j: (i, j, 0))
    return pl.pallas_call(
        functools.partial(_mix_kernel, n_lat=n_lat, rw=rw, heads=heads),
        grid=(b, n_lat),
        in_specs=[dir_spec(0), dir_spec(1), dir_spec(0), dir_spec(1),
                  pl.BlockSpec((1, TOK_TILE, LANE), lambda i, j: (i, j + off, gl_blk)),
                  pl.BlockSpec((1, SUBLANE, LANE), lambda i, j: (i, (j + off) * halo - 1, gl_blk)),
                  pl.BlockSpec((1, SUBLANE, LANE),
                               lambda i, j: (i, jnp.minimum((j + off + 1) * halo, (n_lat + off) * halo - 1), gl_blk)),
                  _const_spec((3, LANE)), _const_spec((LANE, rw)),
                  _const_spec((1, rw)), _const_spec((1, rw)),
                  tok(d - rw), _const_spec((d, d)), _const_spec((1, d)),
                  tok(d),
                  pl.BlockSpec((1, 1, d), lambda i, j: (i, 0, 0)),
                  _const_spec((1, d)),
                  pl.BlockSpec((1, 2, d), lambda i, j: (i, 0, 0))],
        out_specs=[tok(d), tok(d)],
        out_shape=[jax.ShapeDtypeStruct((b, t_lat, d), F32),
                   jax.ShapeDtypeStruct((b, t_lat, d), BF16)],
        compiler_params=_params("parallel", "arbitrary"),
        name="mix_out",
    )(y, y, bon, bon, rwp, rwp, rwp, cwg, g2p, ln_w, ln_b, yd, w_out, g_post, x, gt1, g_pre2, mod2)


def _ffn_kernel(h_ref, hp_ref, hn_ref, wu_ref, cw_ref, cb_ref, wd_ref, x1_ref, gt_ref, gpost_ref,
                o_ref, *, n_tiles, d_ff):
    j = pl.program_id(1)
    h = h_ref[0]
    hp = hp_ref[0]
    hn = hn_ref[0]
    first = j == 0
    last = j == n_tiles - 1
    acc = jnp.zeros(o_ref.shape[1:], F32)
    for c0 in range(0, d_ff, FFN_COLS):
        parts = []
        for base in (c0, d_ff + c0):
            cols = slice(base, base + FFN_COLS)
            w = wu_ref[:, cols]
            u = jnp.dot(h, w, preferred_element_type=F32)
            up = jnp.dot(hp, w, preferred_element_type=F32)[SUBLANE - 1:SUBLANE, :]
            un = jnp.dot(hn, w, preferred_element_type=F32)[0:1, :]
            um1, up1 = _shift_rows(u, jnp.where(first, 0.0, up), jnp.where(last, 0.0, un))
            cw = cw_ref[:, cols]
            parts.append(cw[0:1] * um1 + cw[1:2] * u + cw[2:3] * up1 + cb_ref[:, cols])
        val, gate = parts
        act = (val * gate * _sigmoid(gate)).astype(BF16)
        acc = acc + jnp.dot(act, wd_ref[c0:c0 + FFN_COLS, :], preferred_element_type=F32)
    o_ref[0] = x1_ref[0] + gt_ref[0] * _rms(acc, gpost_ref[...], NORM_EPS)


def _ffn(h2, w_up, conv_w, conv_b, w_down, x1, gt2, g_post):
    b, t_lat, d = x1.shape
    d_ff = w_down.shape[0]
    n_tiles = t_lat // FFN_TILE
    halo = FFN_TILE // SUBLANE
    n_halo = t_lat // SUBLANE
    tok = pl.BlockSpec((1, FFN_TILE, d), lambda i, j: (i, j, 0))
    return pl.pallas_call(
        functools.partial(_ffn_kernel, n_tiles=n_tiles, d_ff=d_ff),
        grid=(b, n_tiles),
        in_specs=[tok,
                  pl.BlockSpec((1, SUBLANE, d), lambda i, j: (i, jnp.maximum(j * halo - 1, 0), 0)),
                  pl.BlockSpec((1, SUBLANE, d), lambda i, j: (i, jnp.minimum((j + 1) * halo, n_halo - 1), 0)),
                  _const_spec(w_up.shape), _const_spec(conv_w.shape), _const_spec(conv_b.shape),
                  _const_spec(w_down.shape),
                  tok,
                  pl.BlockSpec((1, 1, d), lambda i, j: (i, 0, 0)),
                  _const_spec((1, d))],
        out_specs=tok,
        out_shape=jax.ShapeDtypeStruct((b, t_lat, d), F32),
        compiler_params=_params("parallel", "arbitrary"),
        name="conv_ffn",
    )(h2, h2, h2, w_up, conv_w, conv_b, w_down, x1, gt2, g_post)


def _rope_tables(t_ctx, t_lat, d_head):
    n_pair = d_head // 4
    pos = jnp.arange(t_lat, dtype=F32)
    row = jnp.floor(pos / GRID_W)
    col = pos - row * GRID_W
    inv = ROPE_THETA ** (-jnp.arange(n_pair, dtype=F32) / n_pair)
    ang = jnp.concatenate([row[:, None] * inv, col[:, None] * inv], axis=-1)
    ang = jnp.concatenate([jnp.zeros((t_ctx, d_head // 2), F32), ang], axis=0)
    cos, sin = jnp.cos(ang), jnp.sin(ang)
    zero = jnp.zeros_like(sin)
    reps = LANE // d_head
    tile = lambda a, b_: jnp.tile(jnp.concatenate([a, b_], axis=-1), (1, reps))
    return tile(cos, cos), tile(-sin, zero), tile(zero, sin)


def _pad_cols(a, width):
    return jnp.pad(a, ((0, 0), (0, width - a.shape[1])))


def _pad_rows(a, height):
    return jnp.pad(a, ((0, height - a.shape[0]), (0, 0)))


def kernel(x, c, ctx, c_ctx, w_mod, b_mod, g_pre_mix, g_post_mix, g_pre_ffn, g_post_ffn, w_in, rwkv_conv, w0_fwd, w2_fwd, a0_fwd, a2_fwd, w0_bwd, w2_bwd, a0_bwd, a2_bwd, g2, k_k, k_a, r_k, ln_x_w, ln_x_b, lam_q1, lam_k1, lam_q2, lam_k2, subln_w, w_out, w_up, ffn_conv, ffn_conv_b, w_down):
    depth = w_in.shape[0]
    assert depth == 1, "single-layer block only"
    b, t_lat, d = x.shape
    t_ctx = ctx.shape[1]
    assert t_ctx == TOK_TILE and t_lat % FFN_TILE == 0 and t_lat % GRID_W == 0
    rw = k_k.shape[-1]
    heads = r_k.shape[1]
    d_head = lam_q1.shape[-1]
    width = d - rw
    n_lora = (w2_fwd.shape[1], a2_fwd.shape[1], g2.shape[1])
    assert max(n_lora) <= LANE
    lam_init = 0.8 - 0.6 * math.exp(-0.3 * 0)

    cc = _pad_rows(jnp.concatenate([c, c_ctx[None, :]], axis=0), 2 * SUBLANE)
    mod = _modulation(cc, w_mod[0], b_mod)
    sh1, sc1, gt1, sh2, sc2, gt2 = jnp.split(mod[:b], 6, axis=-1)
    sh1c, sc1c = mod[b, 0:d], mod[b, d:2 * d]
    mod1 = jnp.stack([jnp.stack([jnp.broadcast_to(sh1c, (b, d)), jnp.broadcast_to(sc1c, (b, d))], axis=1),
                      jnp.stack([sh1, sc1], axis=1)], axis=1)
    mod2 = jnp.stack([sh2, sc2], axis=1)

    w = w_in[0]
    o = 3 * rw
    cuts = (o, o + n_lora[0], o + n_lora[0] + n_lora[1], o + sum(n_lora))
    n_rw = o + 3 * LANE
    w_p = jnp.concatenate([w[:, :o], _pad_cols(w[:, cuts[0]:cuts[1]], LANE), _pad_cols(w[:, cuts[1]:cuts[2]], LANE),
                           _pad_cols(w[:, cuts[2]:cuts[3]], LANE), w[:, cuts[3]:]], axis=1).astype(BF16)
    cv = rwkv_conv[0]
    cw = jnp.concatenate([cv[:, :o], _pad_cols(cv[:, cuts[0]:cuts[1]], LANE), _pad_cols(cv[:, cuts[1]:cuts[2]], LANE),
                          _pad_cols(cv[:, cuts[2]:cuts[3]], LANE)], axis=1)
    rope_c, rope_s1, rope_s2 = _rope_tables(t_ctx, t_lat, d_head)

    rwp, q, k, v = _inproj(x, ctx, mod1, g_pre_mix, w_p, rope_c, rope_s1, rope_s2, n_rw, width,
                           float(d_head) ** -0.5)

    lam_p = jnp.concatenate([lam_q1, lam_k1, lam_q2, lam_k2], axis=0)
    yd = _attention(q, k, v, lam_p, subln_w, t_lat, lam_init)

    w0 = jnp.stack([w0_fwd, w0_bwd], axis=0)
    a0 = jnp.stack([a0_fwd, a0_bwd], axis=0)
    w2p = jnp.stack([_pad_rows(w2_fwd[0], LANE), _pad_rows(w2_bwd[0], LANE)], axis=0).astype(BF16)
    a2p = jnp.stack([_pad_rows(a2_fwd[0], LANE), _pad_rows(a2_bwd[0], LANE)], axis=0).astype(BF16)
    y, bon = _rwkv_scan(rwp, cw, k_k, k_a, r_k.reshape(1, rw), w0, w2p, a0, a2p, t_lat, rw, heads)

    x1, h2 = _mix_out(y, bon, rwp, cw[:, o + 2 * LANE:], _pad_rows(g2[0], LANE).astype(BF16), ln_x_w, ln_x_b,
                      yd, w_out[0].astype(BF16), g_post_mix, x, gt1[:, None, :], g_pre_ffn, mod2, rw, heads)

    return _ffn(h2, w_up[0].astype(BF16), ffn_conv[0], ffn_conv_b, w_down[0].astype(BF16), x1,
                gt2[:, None, :], g_post_ffn)
```

```python
import functools
import math

import jax
import jax.numpy as jnp
from jax import lax
from jax.experimental import pallas as pl
from jax.experimental.pallas import tpu as pltpu

F32 = jnp.float32
BF16 = jnp.bfloat16

LANE = 128
SUBLANE = 8
TOK_TILE = 256
CHUNK = 64
FFN_TILE = 512
FFN_COLS = 256
SCAN_GROUP_CHUNKS = 2
VMEM_LIMIT = 56 * 1024 * 1024

GRID_W = 64
ROPE_THETA = 10000.0
NORM_EPS = 1e-6
LN_X_EPS = 64e-5
SUBLN_EPS = 1e-5

_NT = (((1,), (1,)), ((), ()))
_TN = (((0,), (0,)), ((), ()))


def _dot(a, b):
    return jnp.dot(a.astype(BF16), b.astype(BF16), preferred_element_type=F32)


def _dot_nt(a, b):
    return lax.dot_general(a.astype(BF16), b.astype(BF16), _NT, preferred_element_type=F32)


def _dot_tn(a, b):
    return lax.dot_general(a.astype(BF16), b.astype(BF16), _TN, preferred_element_type=F32)


def _split(a):
    hi = a.astype(BF16)
    lo = (a - hi.astype(F32)).astype(BF16)
    return hi, lo


def _dot_hl(a, b_exact):
    hi, lo = _split(a)
    return (jnp.dot(hi, b_exact, preferred_element_type=F32)
            + jnp.dot(lo, b_exact, preferred_element_type=F32))


def _dot_lh(a_exact, b):
    hi, lo = _split(b)
    return (jnp.dot(a_exact, hi, preferred_element_type=F32)
            + jnp.dot(a_exact, lo, preferred_element_type=F32))


def _dot3(a, b):
    ah, al = _split(a)
    bh, bl = _split(b)
    return (jnp.dot(ah, bh, preferred_element_type=F32)
            + jnp.dot(ah, bl, preferred_element_type=F32)
            + jnp.dot(al, bh, preferred_element_type=F32))


_mm_gram = _dot_nt
_mm_inv = _dot
_mm_apply = _dot
_mm_tn = _dot_tn
_mm_state = _dot


def _sigmoid(z):
    return 1.0 / (1.0 + jnp.exp(-z))


def _rms(x, g, eps):
    return x * lax.rsqrt(jnp.mean(x * x, axis=-1, keepdims=True) + eps) * g


def _group_ones(width, group):
    i = lax.broadcasted_iota(jnp.int32, (width, width), 0) // group
    j = lax.broadcasted_iota(jnp.int32, (width, width), 1) // group
    return jnp.where(i == j, 1.0, 0.0).astype(BF16)


def _shift_rows(x, prev_row, next_row):
    n = x.shape[0]
    row = lax.broadcasted_iota(jnp.int32, (n, 1), 0)
    xm1 = jnp.where(row == 0, prev_row, pltpu.roll(x, 1, axis=0))
    xp1 = jnp.where(row == n - 1, next_row, pltpu.roll(x, n - 1, axis=0))
    return xm1, xp1


def _params(*sem):
    return pltpu.CompilerParams(dimension_semantics=sem, vmem_limit_bytes=VMEM_LIMIT)


def _const_spec(shape):
    n = len(shape)
    return pl.BlockSpec(shape, lambda *_: (0,) * n, pipeline_mode=pl.Buffered(1))


def _mod_kernel(c_ref, w_ref, b_ref, o_ref):
    c = c_ref[...]
    o_ref[...] = _dot3(c * _sigmoid(c), w_ref[...]) + b_ref[...]


def _modulation(cc, w_mod, b_mod):
    rows, d = cc.shape
    n = w_mod.shape[1]
    tn = 1024
    return pl.pallas_call(
        _mod_kernel,
        grid=(n // tn,),
        in_specs=[pl.BlockSpec((rows, d), lambda j: (0, 0)),
                  pl.BlockSpec((d, tn), lambda j: (0, j)),
                  pl.BlockSpec((1, tn), lambda j: (0, j))],
        out_specs=pl.BlockSpec((rows, tn), lambda j: (0, j)),
        out_shape=jax.ShapeDtypeStruct((rows, n), F32),
        compiler_params=_params("parallel"),
        name="adaln_mod",
    )(cc, w_mod, b_mod)


def _inproj_kernel(x_ref, ctx_ref, mod_ref, g_ref, w_ref, rc_ref, rs1_ref, rs2_ref,
                   rw_ref, q_ref, k_ref, v_ref, *, n_rw, width, scale):
    t = pl.program_id(1)
    xin = jnp.where(t == 0, ctx_ref[0], x_ref[0])
    mod = mod_ref[0, 0]
    h = (_rms(xin, g_ref[...], NORM_EPS) * (1.0 + mod[1:2]) + mod[0:1]).astype(BF16)

    step = 4 * LANE
    for c0 in range(0, n_rw, step):
        c1 = min(c0 + step, n_rw)
        rw_ref[0, :, c0:c1] = jnp.dot(h, w_ref[:, c0:c1], preferred_element_type=F32)

    reps = width // LANE
    cos = jnp.concatenate([rc_ref[...]] * reps, axis=1)
    s1 = jnp.concatenate([rs1_ref[...]] * reps, axis=1)
    s2 = jnp.concatenate([rs2_ref[...]] * reps, axis=1)
    half = GRID_W // 2

    def rope(z):
        return z * cos + pltpu.roll(z, width - half, axis=1) * s1 + pltpu.roll(z, half, axis=1) * s2

    q = jnp.dot(h, w_ref[:, n_rw:n_rw + width], preferred_element_type=F32)
    q_ref[0] = (rope(q) * scale).astype(BF16)
    k = jnp.dot(h, w_ref[:, n_rw + width:n_rw + 2 * width], preferred_element_type=F32)
    k_ref[0] = rope(k).astype(BF16)
    v = jnp.dot(h, w_ref[:, n_rw + 2 * width:n_rw + 3 * width], preferred_element_type=F32)
    v_ref[0] = v.astype(BF16)


def _inproj(x, ctx, mod1, g_pre, w_p, rope_c, rope_s1, rope_s2, n_rw, width, scale):
    b, t_lat, d = x.shape
    n_tiles = (t_lat + ctx.shape[1]) // TOK_TILE
    t_all = n_tiles * TOK_TILE
    tok = lambda w: pl.BlockSpec((1, TOK_TILE, w), lambda i, t: (i, t, 0))
    return pl.pallas_call(
        functools.partial(_inproj_kernel, n_rw=n_rw, width=width, scale=scale),
        grid=(b, n_tiles),
        in_specs=[pl.BlockSpec((1, TOK_TILE, d), lambda i, t: (i, jnp.maximum(t - 1, 0), 0)),
                  pl.BlockSpec((1, TOK_TILE, d), lambda i, t: (i, 0, 0)),
                  pl.BlockSpec((1, 1, 2, d), lambda i, t: (i, jnp.minimum(t, 1), 0, 0)),
                  _const_spec((1, d)),
                  _const_spec(w_p.shape),
                  pl.BlockSpec((TOK_TILE, LANE), lambda i, t: (t, 0)),
                  pl.BlockSpec((TOK_TILE, LANE), lambda i, t: (t, 0)),
                  pl.BlockSpec((TOK_TILE, LANE), lambda i, t: (t, 0))],
        out_specs=[tok(n_rw), tok(width), tok(width), tok(width)],
        out_shape=[jax.ShapeDtypeStruct((b, t_all, n_rw), F32),
                   jax.ShapeDtypeStruct((b, t_all, width), BF16),
                   jax.ShapeDtypeStruct((b, t_all, width), BF16),
                   jax.ShapeDtypeStruct((b, t_all, width), BF16)],
        compiler_params=_params("parallel", "arbitrary"),
        name="in_proj",
    )(x, ctx, mod1, g_pre, w_p, rope_c, rope_s1, rope_s2)


def _attn_kernel(q_ref, k_ref, v_ref, lam_ref, g_ref, o_ref, *, lam_init):
    lp = lam_ref[...]
    lam = (jnp.exp(jnp.sum(lp[0:1] * lp[1:2], axis=-1, keepdims=True))
           - jnp.exp(jnp.sum(lp[2:3] * lp[3:4], axis=-1, keepdims=True)) + lam_init)
    q = q_ref[0]
    k = k_ref[0]
    d = q.shape[-1] // 2
    lane = lax.broadcasted_iota(jnp.int32, q.shape, 1)
    zero = jnp.zeros_like(q)

    def probs(qm):
        s = lax.dot_general(qm, k, _NT, preferred_element_type=F32)
        p = jnp.exp(s - jnp.max(s, axis=-1, keepdims=True))
        return p, 1.0 / jnp.sum(p, axis=-1, keepdims=True)

    p1, r1 = probs(jnp.where(lane < d, q, zero))
    p2, r2 = probs(jnp.where(lane >= d, q, zero))
    a = (p1 * r1 - p2 * (lam * r2)).astype(BF16)
    o = jnp.dot(a, v_ref[0], preferred_element_type=F32)
    o_ref[0] = (_rms(o, g_ref[...], SUBLN_EPS) * (1.0 - lam_init)).astype(BF16)


def _attention(q, k, v, lam_p, subln_w, t_lat, lam_init):
    b, t_all, width = q.shape
    hd = subln_w.shape[-1]
    heads = width // hd
    off = (t_all - t_lat) // TOK_TILE
    return pl.pallas_call(
        functools.partial(_attn_kernel, lam_init=lam_init),
        grid=(b, heads, t_lat // TOK_TILE),
        in_specs=[pl.BlockSpec((1, TOK_TILE, hd), lambda i, h, j: (i, j + off, h)),
                  pl.BlockSpec((1, t_all, hd), lambda i, h, j: (i, 0, h)),
                  pl.BlockSpec((1, t_all, hd), lambda i, h, j: (i, 0, h)),
                  _const_spec(lam_p.shape),
                  _const_spec((1, hd))],
        out_specs=pl.BlockSpec((1, TOK_TILE, hd), lambda i, h, j: (i, j, h)),
        out_shape=jax.ShapeDtypeStruct((b, t_lat, width), BF16),
        compiler_params=_params("parallel", "parallel", "arbitrary"),
        name="diff_attn",
    )(q, k, v, lam_p, subln_w)


def _scan_kernel(main_ref, prev_ref, next_ref, cw_ref, kk_ref, ka_ref, rk_ref,
                 w0_ref, w2_ref, a0_ref, a2_ref,
                 y_ref, bon_ref,
                 h_s, rh_s, kh_s, kt_s, bt_s, kb_s, bb_s, v_s, gam_s,
                 p_s, q_s, r_s, yy_s, *, n_tiles, rw, heads):
    d = pl.program_id(1)
    s = pl.program_id(2)
    fwd = d == 0
    tile = jnp.where(fwd, s, jnp.where(s == 0, 0, n_tiles - s))
    hd = rw // heads
    n_chunks = TOK_TILE // CHUNK

    @pl.when(s == 0)
    def _():
        h_s[...] = jnp.zeros_like(h_s)

    x = main_ref[0]
    starts = (tile == 0) | (tile == 1)
    ends = (tile == 0) | (tile == n_tiles - 1)
    prev_row = jnp.where(starts, 0.0, prev_ref[0][SUBLANE - 1:SUBLANE, :])
    next_row = jnp.where(ends, 0.0, next_ref[0][0:1, :])
    xm1, xp1 = _shift_rows(x, prev_row, next_row)
    cw = cw_ref[...]
    xc = cw[0:1] * xm1 + cw[1:2] * x + cw[2:3] * xp1

    r = xc[:, 0:rw]
    k = xc[:, rw:2 * rw]
    v = xc[:, 2 * rw:3 * rw]
    wl = xc[:, 3 * rw:3 * rw + LANE]
    al = xc[:, 3 * rw + LANE:3 * rw + 2 * LANE]

    g_ones = _group_ones(rw, hd)
    kk = k * kk_ref[...]
    kk = kk / jnp.maximum(jnp.sqrt(_dot_hl(kk * kk, g_ones)), 1e-12)
    z = w0_ref[0] + _dot(jnp.tanh(wl), w2_ref[0])
    w = -(jnp.maximum(-z, 0.0) + jnp.log(1.0 + jnp.exp(-jnp.abs(z)))) - 0.5
    ld = -jnp.exp(w)
    a = _sigmoid(a0_ref[0] + _dot(al, a2_ref[0]))
    kd = k * (1.0 + (a - 1.0) * ka_ref[...])
    bvec = kk * a
    bon_ref[0, 0] = 0.5 * _dot_hl(r * kd * rk_ref[...], g_ones) * v

    ri = lax.broadcasted_iota(jnp.int32, (TOK_TILE, TOK_TILE), 0)
    ci = lax.broadcasted_iota(jnp.int32, (TOK_TILE, TOK_TILE), 1)
    same = (ri // CHUNK) == (ci // CHUNK)
    sign = jnp.where(fwd, 1, -1)
    before = (ri - ci) * sign >= 0
    tri = jnp.where(same & before, 1.0, 0.0).astype(BF16)
    tot = jnp.where(same, 1.0, 0.0).astype(BF16)
    cum = _dot_lh(tri, ld)
    ctot = _dot_lh(tot, ld)
    e_neg = jnp.exp(-cum)
    e_rem = jnp.exp(ctot - cum)
    full = {"rh": r * jnp.exp(cum), "kh": kk * jnp.exp(cum - ld), "kt": kd * e_neg,
            "bt": bvec * e_neg, "kb": kd * e_rem, "bb": bvec * e_rem, "v": v}
    refs = {"rh": rh_s, "kh": kh_s, "kt": kt_s, "bt": bt_s, "kb": kb_s, "bb": bb_s, "v": v_s}
    gam = jnp.exp(ctot)
    for h in range(heads):
        sl = slice(h * hd, (h + 1) * hd)
        for name, val in full.items():
            refs[name][h] = val[:, sl].astype(BF16)
        gam_s[h] = gam[:, sl]

    ii = lax.broadcasted_iota(jnp.int32, (CHUNK, CHUNK), 0)
    jj = lax.broadcasted_iota(jnp.int32, (CHUNK, CHUNK), 1)
    strict = (ii - jj) * sign > 0
    incl = (ii - jj) * sign >= 0
    eye = ii == jj
    eye_f = jnp.where(eye, 1.0, 0.0)
    level_masks = []
    size = 1
    while size < CHUNK:
        level_masks.append(((ii // (2 * size)) == (jj // (2 * size))) & ((ii // size) != (jj // size)))
        size *= 2

    def build_group(chains):
        n = range(len(chains))
        rows = [slice(c * CHUNK, (c + 1) * CHUNK) for _, c in chains]
        ld = lambda ref: [ref[h, rows[i], :] for i, (h, _) in enumerate(chains)]
        rh, kh, kt, bt, kb, bb, vv = (ld(r) for r in (rh_s, kh_s, kt_s, bt_s, kb_s, bb_s, v_s))
        g_row = [gam_s[h, c * CHUNK:c * CHUNK + 1, :] for h, c in chains]
        m_ab = [jnp.where(strict, _mm_gram(kh[i], bt[i]), 0.0) for i in n]
        m_ak = [jnp.where(strict, _mm_gram(kh[i], kt[i]), 0.0) for i in n]
        a_rk = [jnp.where(incl, _mm_gram(rh[i], kt[i]), 0.0) for i in n]
        a_rb = [jnp.where(incl, _mm_gram(rh[i], bt[i]), 0.0) for i in n]
        tinv = [eye_f - jnp.where(level_masks[0], m_ab[i], 0.0) for i in n]
        for lm in level_masks[1:]:
            right = [_mm_inv(jnp.where(lm, m_ab[i], 0.0), tinv[i]) for i in n]
            tinv = [tinv[i] - _mm_inv(tinv[i], right[i]) for i in n]
        mv = [_mm_apply(m_ak[i], vv[i]) for i in n]
        wmat = [_mm_apply(tinv[i], kh[i]) for i in n]
        umat = [_mm_apply(tinv[i], mv[i]) for i in n]
        for i, (h, c) in enumerate(chains):
            idx = h * n_chunks + c
            p_s[idx] = jnp.where(eye, g_row[i], 0.0) - _mm_tn(bb[i], wmat[i])
            q_s[idx] = _mm_tn(kb[i], vv[i]) - _mm_tn(bb[i], umat[i])
            r_s[idx] = rh[i].astype(F32) - _mm_apply(a_rb[i], wmat[i])
            yy_s[idx] = _mm_apply(a_rk[i], vv[i]) - _mm_apply(a_rb[i], umat[i])

    for c0 in range(0, n_chunks, SCAN_GROUP_CHUNKS):
        build_group([(h, c) for c in range(c0, c0 + SCAN_GROUP_CHUNKS) for h in range(heads)])

    def chain(order):
        state = [h_s[h] for h in range(heads)]
        for c in order:
            ids = [h * n_chunks + c for h in range(heads)]
            ys = [_mm_state(r_s[ids[h]], state[h]) + yy_s[ids[h]] for h in range(heads)]
            state = [_mm_state(p_s[ids[h]], state[h]) + q_s[ids[h]] for h in range(heads)]
            for h in range(heads):
                y_ref[0, 0, c * CHUNK:(c + 1) * CHUNK, h * hd:(h + 1) * hd] = ys[h]
        for h in range(heads):
            h_s[h] = state[h]

    @pl.when(fwd)
    def _():
        chain(range(n_chunks))

    @pl.when(jnp.logical_not(fwd))
    def _():
        chain(range(n_chunks - 1, -1, -1))


def _rwkv_scan(rwp, cw, k_k, k_a, r_k, w0, w2p, a0, a2p, t_lat, rw, heads):
    b, t_all, n_rw = rwp.shape
    n_tiles = t_all // TOK_TILE
    n_lat = t_lat // TOK_TILE
    hd = rw // heads
    halo = TOK_TILE // SUBLANE
    n_halo = t_all // SUBLANE

    def tile_of(d, s):
        return jnp.where(d == 0, s, jnp.where(s == 0, 0, n_tiles - s))

    def out_block(d, s):
        return jnp.clip(tile_of(d, s) - 1 + jnp.where((d == 1) & (s == 0), n_lat, 0), 0, n_lat - 1)

    out_spec = pl.BlockSpec((1, 1, TOK_TILE, rw), lambda i, d, s: (d, i, out_block(d, s), 0))
    dir_vec = pl.BlockSpec((1, 1, rw), lambda i, d, s: (d, 0, 0))
    dir_mat = pl.BlockSpec((1, LANE, rw), lambda i, d, s: (d, 0, 0))
    n_idx = heads * (TOK_TILE // CHUNK)
    head_bf = pltpu.VMEM((heads, TOK_TILE, hd), BF16)
    head_f32 = pltpu.VMEM((heads, TOK_TILE, hd), F32)
    mat = pltpu.VMEM((n_idx, CHUNK, hd), F32)
    return pl.pallas_call(
        functools.partial(_scan_kernel, n_tiles=n_tiles, rw=rw, heads=heads),
        grid=(b, 2, n_tiles),
        in_specs=[pl.BlockSpec((1, TOK_TILE, n_rw), lambda i, d, s: (i, tile_of(d, s), 0)),
                  pl.BlockSpec((1, SUBLANE, n_rw),
                               lambda i, d, s: (i, jnp.maximum(tile_of(d, s) * halo - 1, 0), 0)),
                  pl.BlockSpec((1, SUBLANE, n_rw),
                               lambda i, d, s: (i, jnp.minimum((tile_of(d, s) + 1) * halo, n_halo - 1), 0)),
                  _const_spec(cw.shape), _const_spec((1, rw)), _const_spec((1, rw)), _const_spec((1, rw)),
                  dir_vec, dir_mat, dir_vec, dir_mat],
        out_specs=[out_spec, out_spec],
        out_shape=[jax.ShapeDtypeStruct((2, b, t_lat, rw), F32),
                   jax.ShapeDtypeStruct((2, b, t_lat, rw), F32)],
        scratch_shapes=[pltpu.VMEM((heads, hd, hd), F32),
                        head_bf, head_bf, head_bf, head_bf, head_bf, head_bf, head_bf, head_f32,
                        mat, mat, mat, mat],
        compiler_params=_params("parallel", "arbitrary", "arbitrary"),
        name="rwkv_scan",
    )(rwp, rwp, rwp, cw, k_k, k_a, r_k, w0, w2p, a0, a2p)


def _mix_kernel(yf_ref, yb_ref, bf_ref, bb_ref, gl_ref, glp_ref, gln_ref, cwg_ref, g2_ref,
                lnw_ref, lnb_ref, yd_ref, wo_ref, gpost_ref, x_ref, gt_ref, gpre_ref, mod_ref,
                x1_ref, h2_ref, *, n_lat, rw, heads):
    i = pl.program_id(1)
    hd = rw // heads
    g_ones = _group_ones(rw, hd)
    y = yf_ref[0, 0] + yb_ref[0, 0]
    mu = _dot_hl(y, g_ones) * (1.0 / hd)
    yc = y - mu
    var = _dot_hl(yc * yc, g_ones) * (1.0 / hd)
    yn = yc * lax.rsqrt(var + LN_X_EPS) * lnw_ref[...] + lnb_ref[...]

    gl = gl_ref[0]
    prev_row = jnp.where(i == 0, 0.0, glp_ref[0][SUBLANE - 1:SUBLANE, :])
    next_row = jnp.where(i == n_lat - 1, 0.0, gln_ref[0][0:1, :])
    gm1, gp1 = _shift_rows(gl, prev_row, next_row)
    cw = cwg_ref[...]
    glc = cw[0:1] * gm1 + cw[1:2] * gl + cw[2:3] * gp1
    gate = _dot(_sigmoid(glc), g2_ref[...])

    yr = ((yn + bf_ref[0, 0] + bb_ref[0, 0]) * gate).astype(BF16)
    m = (jnp.dot(yr, wo_ref[0:rw, :], preferred_element_type=F32)
         + jnp.dot(yd_ref[0], wo_ref[rw:, :], preferred_element_type=F32))
    x1 = x_ref[0] + gt_ref[0] * _rms(m, gpost_ref[...], NORM_EPS)
    x1_ref[0] = x1
    mod = mod_ref[0]
    h2_ref[0] = (_rms(x1, gpre_ref[...], NORM_EPS) * (1.0 + mod[1:2]) + mod[0:1]).astype(BF16)


def _mix_out(y, bon, rwp, cwg, g2p, ln_w, ln_b, yd, w_out, g_post, x, gt1, g_pre2, mod2, rw, heads):
    b, t_lat, d = x.shape
    n_lat = t_lat // TOK_TILE
    off = (rwp.shape[1] - t_lat) // TOK_TILE
    gl_blk = (3 * rw + 2 * LANE) // LANE
    halo = TOK_TILE // SUBLANE
    dir_spec = lambda dd: pl.BlockSpec((1, 1, TOK_TILE, rw), lambda i, j: (dd, i, j, 0))
    tok = lambda w: pl.BlockSpec((1, TOK_TILE, w), lambda i, j: (i, j, 0))
    return pl.pallas_call(
        functools.partial(_mix_kernel, n_lat=n_lat, rw=rw, heads=heads),
        grid=(b, n_lat),
        in_specs=[dir_spec(0), dir_spec(1), dir_spec(0), dir_spec(1),
                  pl.BlockSpec((1, TOK_TILE, LANE), lambda i, j: (i, j + off, gl_blk)),
                  pl.BlockSpec((1, SUBLANE, LANE), lambda i, j: (i, (j + off) * halo - 1, gl_blk)),
                  pl.BlockSpec((1, SUBLANE, LANE),
                               lambda i, j: (i, jnp.minimum((j + off + 1) * halo, (n_lat + off) * halo - 1), gl_blk)),
                  _const_spec((3, LANE)), _const_spec((LANE, rw)),
                  _const_spec((1, rw)), _const_spec((1, rw)),
                  tok(d - rw), _const_spec((d, d)), _const_spec((1, d)),
                  tok(d),
                  pl.BlockSpec((1, 1, d), lambda i, j: (i, 0, 0)),
                  _const_spec((1, d)),
                  pl.BlockSpec((1, 2, d), lambda i, j: (i, 0, 0))],
        out_specs=[tok(d), tok(d)],
        out_shape=[jax.ShapeDtypeStruct((b, t_lat, d), F32),
                   jax.ShapeDtypeStruct((b, t_lat, d), BF16)],
        compiler_params=_params("parallel", "arbitrary"),
        name="mix_out",
    )(y, y, bon, bon, rwp, rwp, rwp, cwg, g2p, ln_w, ln_b, yd, w_out, g_post, x, gt1, g_pre2, mod2)


def _ffn_kernel(h_ref, hp_ref, hn_ref, wu_ref, cw_ref, cb_ref, wd_ref, x1_ref, gt_ref, gpost_ref,
                o_ref, *, n_tiles, d_ff):
    j = pl.program_id(1)
    h = h_ref[0]
    hp = hp_ref[0]
    hn = hn_ref[0]
    first = j == 0
    last = j == n_tiles - 1
    acc = jnp.zeros(o_ref.shape[1:], F32)
    for c0 in range(0, d_ff, FFN_COLS):
        parts = []
        for base in (c0, d_ff + c0):
            cols = slice(base, base + FFN_COLS)
            w = wu_ref[:, cols]
            u = jnp.dot(h, w, preferred_element_type=F32)
            up = jnp.dot(hp, w, preferred_element_type=F32)[SUBLANE - 1:SUBLANE, :]
            un = jnp.dot(hn, w, preferred_element_type=F32)[0:1, :]
            um1, up1 = _shift_rows(u, jnp.where(first, 0.0, up), jnp.where(last, 0.0, un))
            cw = cw_ref[:, cols]
            parts.append(cw[0:1] * um1 + cw[1:2] * u + cw[2:3] * up1 + cb_ref[:, cols])
        val, gate = parts
        act = (val * gate * _sigmoid(gate)).astype(BF16)
        acc = acc + jnp.dot(act, wd_ref[c0:c0 + FFN_COLS, :], preferred_element_type=F32)
    o_ref[0] = x1_ref[0] + gt_ref[0] * _rms(acc, gpost_ref[...], NORM_EPS)


def _ffn(h2, w_up, conv_w, conv_b, w_down, x1, gt2, g_post):
    b, t_lat, d = x1.shape
    d_ff = w_down.shape[0]
    n_tiles = t_lat // FFN_TILE
    halo = FFN_TILE // SUBLANE
    n_halo = t_lat // SUBLANE
    tok = pl.BlockSpec((1, FFN_TILE, d), lambda i, j: (i, j, 0))
    return pl.pallas_call(
        functools.partial(_ffn_kernel, n_tiles=n_tiles, d_ff=d_ff),
        grid=(b, n_tiles),
        in_specs=[tok,
                  pl.BlockSpec((1, SUBLANE, d), lambda i, j: (i, jnp.maximum(j * halo - 1, 0), 0)),
                  pl.BlockSpec((1, SUBLANE, d), lambda i, j: (i, jnp.minimum((j + 1) * halo, n_halo - 1), 0)),
                  _const_spec(w_up.shape), _const_spec(conv_w.shape), _const_spec(conv_b.shape),
                  _const_spec(w_down.shape),
                  tok,
                  pl.BlockSpec((1, 1, d), lambda i, j: (i, 0, 0)),
                  _const_spec((1, d))],
        out_specs=tok,
        out_shape=jax.ShapeDtypeStruct((b, t_lat, d), F32),
        compiler_params=_params("parallel", "arbitrary"),
        name="conv_ffn",
    )(h2, h2, h2, w_up, conv_w, conv_b, w_down, x1, gt2, g_post)


def _rope_tables(t_ctx, t_lat, d_head):
    n_pair = d_head // 4
    pos = jnp.arange(t_lat, dtype=F32)
    row = jnp.floor(pos / GRID_W)
    col = pos - row * GRID_W
    inv = ROPE_THETA ** (-jnp.arange(n_pair, dtype=F32) / n_pair)
    ang = jnp.concatenate([row[:, None] * inv, col[:, None] * inv], axis=-1)
    ang = jnp.concatenate([jnp.zeros((t_ctx, d_head // 2), F32), ang], axis=0)
    cos, sin = jnp.cos(ang), jnp.sin(ang)
    zero = jnp.zeros_like(sin)
    reps = LANE // d_head
    tile = lambda a, b_: jnp.tile(jnp.concatenate([a, b_], axis=-1), (1, reps))
    return tile(cos, cos), tile(-sin, zero), tile(zero, sin)


def _pad_cols(a, width):
    return jnp.pad(a, ((0, 0), (0, width - a.shape[1])))


def _pad_rows(a, height):
    return jnp.pad(a, ((0, height - a.shape[0]), (0, 0)))


def kernel(x, c, ctx, c_ctx, w_mod, b_mod, g_pre_mix, g_post_mix, g_pre_ffn, g_post_ffn, w_in, rwkv_conv, w0_fwd, w2_fwd, a0_fwd, a2_fwd, w0_bwd, w2_bwd, a0_bwd, a2_bwd, g2, k_k, k_a, r_k, ln_x_w, ln_x_b, lam_q1, lam_k1, lam_q2, lam_k2, subln_w, w_out, w_up, ffn_conv, ffn_conv_b, w_down):
    depth = w_in.shape[0]
    assert depth == 1, "single-layer block only"
    b, t_lat, d = x.shape
    t_ctx = ctx.shape[1]
    assert t_ctx == TOK_TILE and t_lat % FFN_TILE == 0 and t_lat % GRID_W == 0
    rw = k_k.shape[-1]
    heads = r_k.shape[1]
    d_head = lam_q1.shape[-1]
    width = d - rw
    n_lora = (w2_fwd.shape[1], a2_fwd.shape[1], g2.shape[1])
    assert max(n_lora) <= LANE
    lam_init = 0.8 - 0.6 * math.exp(-0.3 * 0)

    cc = _pad_rows(jnp.concatenate([c, c_ctx[None, :]], axis=0), 2 * SUBLANE)
    mod = _modulation(cc, w_mod[0], b_mod)
    sh1, sc1, gt1, sh2, sc2, gt2 = jnp.split(mod[:b], 6, axis=-1)
    sh1c, sc1c = mod[b, 0:d], mod[b, d:2 * d]
    mod1 = jnp.stack([jnp.stack([jnp.broadcast_to(sh1c, (b, d)), jnp.broadcast_to(sc1c, (b, d))], axis=1),
                      jnp.stack([sh1, sc1], axis=1)], axis=1)
    mod2 = jnp.stack([sh2, sc2], axis=1)

    w = w_in[0]
    o = 3 * rw
    cuts = (o, o + n_lora[0], o + n_lora[0] + n_lora[1], o + sum(n_lora))
    n_rw = o + 3 * LANE
    w_p = jnp.concatenate([w[:, :o], _pad_cols(w[:, cuts[0]:cuts[1]], LANE), _pad_cols(w[:, cuts[1]:cuts[2]], LANE),
                           _pad_cols(w[:, cuts[2]:cuts[3]], LANE), w[:, cuts[3]:]], axis=1).astype(BF16)
    cv = rwkv_conv[0]
    cw = jnp.concatenate([cv[:, :o], _pad_cols(cv[:, cuts[0]:cuts[1]], LANE), _pad_cols(cv[:, cuts[1]:cuts[2]], LANE),
                          _pad_cols(cv[:, cuts[2]:cuts[3]], LANE)], axis=1)
    rope_c, rope_s1, rope_s2 = _rope_tables(t_ctx, t_lat, d_head)

    rwp, q, k, v = _inproj(x, ctx, mod1, g_pre_mix, w_p, rope_c, rope_s1, rope_s2, n_rw, width,
                           float(d_head) ** -0.5)

    lam_p = jnp.concatenate([lam_q1, lam_k1, lam_q2, lam_k2], axis=0)
    yd = _attention(q, k, v, lam_p, subln_w, t_lat, lam_init)

    w0 = jnp.stack([w0_fwd, w0_bwd], axis=0)
    a0 = jnp.stack([a0_fwd, a0_bwd], axis=0)
    w2p = jnp.stack([_pad_rows(w2_fwd[0], LANE), _pad_rows(w2_bwd[0], LANE)], axis=0).astype(BF16)
    a2p = jnp.stack([_pad_rows(a2_fwd[0], LANE), _pad_rows(a2_bwd[0], LANE)], axis=0).astype(BF16)
    y, bon = _rwkv_scan(rwp, cw, k_k, k_a, r_k.reshape(1, rw), w0, w2p, a0, a2p, t_lat, rw, heads)

    x1, h2 = _mix_out(y, bon, rwp, cw[:, o + 2 * LANE:], _pad_rows(g2[0], LANE).astype(BF16), ln_x_w, ln_x_b,
                      yd, w_out[0].astype(BF16), g_post_mix, x, gt1[:, None, :], g_pre_ffn, mod2, rw, heads)

    return _ffn(h2, w_up[0].astype(BF16), ffn_conv[0], ffn_conv_b, w_down[0].astype(BF16), x1,
                gt2[:, None, :], g_post_ffn)
```

```python
import functools
import math

import jax
import jax.numpy as jnp
from jax import lax
from jax.experimental import pallas as pl
from jax.experimental.pallas import tpu as pltpu

F32 = jnp.float32
BF16 = jnp.bfloat16

LANE = 128
SUBLANE = 8
MXU_WIDTH = 256
TOK_TILE = 256
CHUNK = 64
ATTN_TILE = 512
ATTN_SUB = 256
FFN_TILE = 1024
FFN_HALO = 16
FFN_COLS = 256
SCAN_GROUP_CHUNKS = 4
VMEM_LIMIT = 56 * 1024 * 1024

GRID_W = 64
ROPE_THETA = 10000.0
NORM_EPS = 1e-6
LN_X_EPS = 64e-5
SUBLN_EPS = 1e-5

_NT = (((1,), (1,)), ((), ()))
_TN = (((0,), (0,)), ((), ()))


def _dot(a, b):
    return jnp.dot(a.astype(BF16), b.astype(BF16), preferred_element_type=F32)


def _dot_nt(a, b):
    return lax.dot_general(a.astype(BF16), b.astype(BF16), _NT, preferred_element_type=F32)


def _dot_tn(a, b):
    return lax.dot_general(a.astype(BF16), b.astype(BF16), _TN, preferred_element_type=F32)


def _split(a):
    hi = a.astype(BF16)
    lo = (a - hi.astype(F32)).astype(BF16)
    return hi, lo


def _dot_hl(a, b_exact):
    hi, lo = _split(a)
    return (jnp.dot(hi, b_exact, preferred_element_type=F32)
            + jnp.dot(lo, b_exact, preferred_element_type=F32))


def _dot_lh(a_exact, b):
    hi, lo = _split(b)
    return (jnp.dot(a_exact, hi, preferred_element_type=F32)
            + jnp.dot(a_exact, lo, preferred_element_type=F32))


def _dot3(a, b):
    ah, al = _split(a)
    bh, bl = _split(b)
    return (jnp.dot(ah, bh, preferred_element_type=F32)
            + jnp.dot(ah, bl, preferred_element_type=F32)
            + jnp.dot(al, bh, preferred_element_type=F32))


def _sigmoid(z):
    return 1.0 / (1.0 + jnp.exp(-z))


def _rms(x, g, eps):
    return x * lax.rsqrt(jnp.mean(x * x, axis=-1, keepdims=True) + eps) * g


def _group_ones(width, group):
    i = lax.broadcasted_iota(jnp.int32, (width, width), 0) // group
    j = lax.broadcasted_iota(jnp.int32, (width, width), 1) // group
    return jnp.where(i == j, 1.0, 0.0).astype(BF16)


def _group_sum(x, group, exact=False):
    ones = _group_ones(MXU_WIDTH, group)
    pieces = []
    for c0 in range(0, x.shape[1], MXU_WIDTH):
        piece = x[:, c0:c0 + MXU_WIDTH]
        pieces.append(_dot_hl(piece, ones) if exact else _dot(piece, ones))
    return jnp.concatenate(pieces, axis=1)


def _shift_rows(x, prev_row, next_row):
    n = x.shape[0]
    row = lax.broadcasted_iota(jnp.int32, (n, 1), 0)
    xm1 = jnp.where(row == 0, prev_row, pltpu.roll(x, 1, axis=0))
    xp1 = jnp.where(row == n - 1, next_row, pltpu.roll(x, n - 1, axis=0))
    return xm1, xp1


def _params(*sem):
    return pltpu.CompilerParams(dimension_semantics=sem, vmem_limit_bytes=VMEM_LIMIT)


def _const_spec(shape):
    n = len(shape)
    return pl.BlockSpec(shape, lambda *_: (0,) * n, pipeline_mode=pl.Buffered(1))


def _mod_kernel(c_ref, w_ref, b_ref, o_ref):
    c = c_ref[...]
    o_ref[...] = _dot3(c * _sigmoid(c), w_ref[...]) + b_ref[...]


def _modulation(cc, w_mod, b_mod):
    rows, d = cc.shape
    n = w_mod.shape[1]
    tn = 1024
    return pl.pallas_call(
        _mod_kernel,
        grid=(n // tn,),
        in_specs=[pl.BlockSpec((rows, d), lambda j: (0, 0)),
                  pl.BlockSpec((d, tn), lambda j: (0, j)),
                  pl.BlockSpec((1, tn), lambda j: (0, j))],
        out_specs=pl.BlockSpec((rows, tn), lambda j: (0, j)),
        out_shape=jax.ShapeDtypeStruct((rows, n), F32),
        compiler_params=_params("parallel"),
        name="adaln_mod",
    )(cc, w_mod, b_mod)


def _inproj_kernel(x_ref, ctx_ref, mod_ref, g_ref, w_ref, rc_ref, rs1_ref, rs2_ref,
                   rw_ref, q_ref, k_ref, v_ref, *, n_rw, width, scale):
    t = pl.program_id(1)
    xin = jnp.where(t == 0, ctx_ref[0], x_ref[0])
    mod = mod_ref[0, 0]
    h = (_rms(xin, g_ref[...], NORM_EPS) * (1.0 + mod[1:2]) + mod[0:1]).astype(BF16)

    step = 4 * LANE
    for c0 in range(0, n_rw, step):
        c1 = min(c0 + step, n_rw)
        rw_ref[0, :, c0:c1] = jnp.dot(h, w_ref[:, c0:c1], preferred_element_type=F32)

    reps = width // LANE
    cos = jnp.concatenate([rc_ref[...]] * reps, axis=1)
    s1 = jnp.concatenate([rs1_ref[...]] * reps, axis=1)
    s2 = jnp.concatenate([rs2_ref[...]] * reps, axis=1)
    half = GRID_W // 2

    def rope(z):
        return z * cos + pltpu.roll(z, width - half, axis=1) * s1 + pltpu.roll(z, half, axis=1) * s2

    q = jnp.dot(h, w_ref[:, n_rw:n_rw + width], preferred_element_type=F32)
    q_ref[0] = (rope(q) * scale).astype(BF16)
    k = jnp.dot(h, w_ref[:, n_rw + width:n_rw + 2 * width], preferred_element_type=F32)
    k_ref[0] = rope(k).astype(BF16)
    v = jnp.dot(h, w_ref[:, n_rw + 2 * width:n_rw + 3 * width], preferred_element_type=F32)
    v_ref[0] = v.astype(BF16)


def _inproj(x, ctx, mod1, g_pre, w_p, rope_c, rope_s1, rope_s2, n_rw, width, scale):
    b, t_lat, d = x.shape
    n_ctx = ctx.shape[1] // TOK_TILE
    n_tiles = t_lat // TOK_TILE + n_ctx
    t_all = n_tiles * TOK_TILE
    tok = lambda w: pl.BlockSpec((1, TOK_TILE, w), lambda i, t: (i, t, 0))
    return pl.pallas_call(
        functools.partial(_inproj_kernel, n_rw=n_rw, width=width, scale=scale),
        grid=(b, n_tiles),
        in_specs=[pl.BlockSpec((1, TOK_TILE, d), lambda i, t: (i, jnp.maximum(t - 1, 0), 0)),
                  pl.BlockSpec((1, TOK_TILE, d), lambda i, t: (i, 0, 0)),
                  pl.BlockSpec((1, 1, 2, d), lambda i, t: (i, jnp.minimum(t, 1), 0, 0)),
                  _const_spec((1, d)),
                  _const_spec(w_p.shape),
                  pl.BlockSpec((TOK_TILE, LANE), lambda i, t: (t, 0)),
                  pl.BlockSpec((TOK_TILE, LANE), lambda i, t: (t, 0)),
                  pl.BlockSpec((TOK_TILE, LANE), lambda i, t: (t, 0))],
        out_specs=[tok(n_rw),
                   pl.BlockSpec((1, TOK_TILE, width), lambda i, t: (i, jnp.maximum(t - n_ctx, 0), 0)),
                   tok(width), tok(width)],
        out_shape=[jax.ShapeDtypeStruct((b, t_all, n_rw), F32),
                   jax.ShapeDtypeStruct((b, t_lat, width), BF16),
                   jax.ShapeDtypeStruct((b, t_all, width), BF16),
                   jax.ShapeDtypeStruct((b, t_all, width), BF16)],
        compiler_params=_params("parallel", "arbitrary"),
        name="in_proj",
    )(x, ctx, mod1, g_pre, w_p, rope_c, rope_s1, rope_s2)


def _attn_kernel(q_ref, k_ref, v_ref, lam_ref, g_ref, o_ref, *, lam_init):
    lp = lam_ref[...]
    lam = (jnp.exp(jnp.sum(lp[0:1] * lp[1:2], axis=-1, keepdims=True))
           - jnp.exp(jnp.sum(lp[2:3] * lp[3:4], axis=-1, keepdims=True)) + lam_init)
    k = k_ref[0]
    d = k.shape[-1] // 2
    lane = lax.broadcasted_iota(jnp.int32, (ATTN_SUB, 2 * d), 1)
    zero = jnp.zeros((ATTN_SUB, 2 * d), BF16)

    scores = []
    for i in range(ATTN_TILE // ATTN_SUB):
        q = q_ref[0, i * ATTN_SUB:(i + 1) * ATTN_SUB, :]
        scores.append([lax.dot_general(jnp.where(sel, q, zero), k, _NT, preferred_element_type=F32)
                       for sel in (lane < d, lane >= d)])
    for i, (s1, s2) in enumerate(scores):
        outs = []
        for s in (s1, s2):
            p = jnp.exp2(s - jnp.max(s, axis=-1, keepdims=True))
            r = 1.0 / jnp.sum(p, axis=-1, keepdims=True)
            outs.append(jnp.dot(p.astype(BF16), v_ref[0], preferred_element_type=F32) * r)
        o = outs[0] - lam * outs[1]
        o_ref[0, i * ATTN_SUB:(i + 1) * ATTN_SUB, :] = (
            _rms(o, g_ref[...], SUBLN_EPS) * (1.0 - lam_init)).astype(BF16)


def _attention(q, k, v, lam_p, subln_w, t_lat, lam_init):
    b, t_all, width = k.shape
    hd = subln_w.shape[-1]
    heads = width // hd
    return pl.pallas_call(
        functools.partial(_attn_kernel, lam_init=lam_init),
        grid=(b, heads, t_lat // ATTN_TILE),
        in_specs=[pl.BlockSpec((1, ATTN_TILE, hd), lambda i, h, j: (i, j, h)),
                  pl.BlockSpec((1, t_all, hd), lambda i, h, j: (i, 0, h)),
                  pl.BlockSpec((1, t_all, hd), lambda i, h, j: (i, 0, h)),
                  _const_spec(lam_p.shape),
                  _const_spec((1, hd))],
        out_specs=pl.BlockSpec((1, ATTN_TILE, hd), lambda i, h, j: (i, j, h)),
        out_shape=jax.ShapeDtypeStruct((b, t_lat, width), BF16),
        compiler_params=_params("parallel", "parallel", "arbitrary"),
        name="diff_attn",
    )(q, k, v, lam_p, subln_w)


def _scan_kernel(main_ref, prev_ref, next_ref, cw_ref, kk_ref, ka_ref, rk_ref,
                 w0_ref, w2_ref, a0_ref, a2_ref,
                 y_ref, bon_ref,
                 h_s, khz_s, rhz_s, vz_s, gamz_s, kt_s, bt_s, kb_s, bb_s, rp_s, *, n_tiles, rw, heads):
    d = pl.program_id(1)
    s = pl.program_id(2)
    fwd = d == 0
    tile = jnp.where(fwd, s, jnp.where(s == 0, 0, n_tiles - s))
    hd = rw // heads
    n_chunks = TOK_TILE // CHUNK

    @pl.when(s == 0)
    def _():
        h_s[...] = jnp.zeros_like(h_s)

    x = main_ref[0]
    starts = (tile == 0) | (tile == 1)
    ends = (tile == 0) | (tile == n_tiles - 1)
    prev_row = jnp.where(starts, 0.0, prev_ref[0][SUBLANE - 1:SUBLANE, :])
    next_row = jnp.where(ends, 0.0, next_ref[0][0:1, :])
    xm1, xp1 = _shift_rows(x, prev_row, next_row)
    cw = cw_ref[...]
    xc = cw[0:1] * xm1 + cw[1:2] * x + cw[2:3] * xp1

    r = xc[:, 0:rw]
    k = xc[:, rw:2 * rw]
    v = xc[:, 2 * rw:3 * rw]
    wl = xc[:, 3 * rw:3 * rw + LANE]
    al = xc[:, 3 * rw + LANE:3 * rw + 2 * LANE]

    kk = k * kk_ref[...]
    kk = kk / jnp.maximum(jnp.sqrt(_group_sum(kk * kk, hd)), 1e-12)
    z = w0_ref[0] + _dot(jnp.tanh(wl), w2_ref[0])
    w = -(jnp.maximum(-z, 0.0) + jnp.log(1.0 + jnp.exp(-jnp.abs(z)))) - 0.5
    ld = -jnp.exp(w)
    a = _sigmoid(a0_ref[0] + _dot(al, a2_ref[0]))
    kd = k * (1.0 + (a - 1.0) * ka_ref[...])
    bvec = kk * a
    bon_ref[0, 0] = 0.5 * _group_sum(r * kd * rk_ref[...], hd) * v

    ri = lax.broadcasted_iota(jnp.int32, (TOK_TILE, TOK_TILE), 0)
    ci = lax.broadcasted_iota(jnp.int32, (TOK_TILE, TOK_TILE), 1)
    same = (ri // CHUNK) == (ci // CHUNK)
    sign = jnp.where(fwd, 1, -1)
    before = (ri - ci) * sign >= 0
    tri = jnp.where(same & before, 1.0, 0.0).astype(BF16)
    cum = _dot_lh(tri, ld)
    ctot = jnp.concatenate(
        [jnp.broadcast_to(jnp.where(fwd, cum[c * CHUNK + CHUNK - 1:(c + 1) * CHUNK], cum[c * CHUNK:c * CHUNK + 1]),
                          (CHUNK, rw)) for c in range(n_chunks)], axis=0)
    e_neg = jnp.exp(-cum)
    e_rem = jnp.exp(ctot - cum)
    gam = jnp.exp(jnp.concatenate([ctot[c * CHUNK:c * CHUNK + SUBLANE] for c in range(n_chunks)], axis=0))
    narrow = ((kt_s, kd * e_neg), (bt_s, bvec * e_neg), (kb_s, kd * e_rem), (bb_s, bvec * e_rem))
    wide_lo = ((khz_s, kk * jnp.exp(cum - ld)), (rhz_s, r * jnp.exp(cum)), (gamz_s, gam))
    for h in range(heads):
        sl = slice(h * hd, (h + 1) * hd)
        for ref, val in narrow:
            ref[h] = val[:, sl].astype(BF16)
        for ref, val in wide_lo:
            lo = val[:, sl]
            ref[h] = jnp.concatenate([lo, jnp.zeros_like(lo)], axis=1).astype(ref.dtype)
        vz_s[h] = jnp.concatenate([jnp.zeros((TOK_TILE, hd), F32), v[:, sl]], axis=1).astype(BF16)

    ii = lax.broadcasted_iota(jnp.int32, (CHUNK, CHUNK), 0)
    jj = lax.broadcasted_iota(jnp.int32, (CHUNK, CHUNK), 1)
    eye_f = jnp.where(ii == jj, 1.0, 0.0)
    level_masks = []
    size = 1
    while size < CHUNK:
        level_masks.append(((ii // (2 * size)) == (jj // (2 * size))) & ((ii // size) != (jj // size)))
        size *= 2
    iw = lax.broadcasted_iota(jnp.int32, (CHUNK, 2 * CHUNK), 0)
    jw = lax.broadcasted_iota(jnp.int32, (CHUNK, 2 * CHUNK), 1)
    jw = jnp.where(jw >= CHUNK, jw - CHUNK, jw)
    strict = (iw - jw) * sign > 0
    incl = (iw - jw) * sign >= 0
    eye_wide = (lax.broadcasted_iota(jnp.int32, (hd, 2 * hd), 0)
                == lax.broadcasted_iota(jnp.int32, (hd, 2 * hd), 1))
    eye_b = eye_f.astype(BF16)

    def build_group(chains):
        n = range(len(chains))
        rows = [slice(c * CHUNK, (c + 1) * CHUNK) for _, c in chains]
        ld = lambda ref: [ref[h, rows[i], :] for i, (h, _) in enumerate(chains)]
        khz, rhz, vz, kt, bt, kb, bb = (ld(r) for r in (khz_s, rhz_s, vz_s, kt_s, bt_s, kb_s, bb_s))
        g_row = [gamz_s[h, c * SUBLANE:c * SUBLANE + 1, :] for h, c in chains]
        gram = [_dot_nt(jnp.concatenate([khz[i][:, :hd], rhz[i][:, :hd]], axis=0),
                        jnp.concatenate([bt[i], kt[i]], axis=0)) for i in n]
        mm = [jnp.where(strict, gram[i][:CHUNK], 0.0) for i in n]
        aa = [jnp.where(incl, gram[i][CHUNK:], 0.0) for i in n]
        m_ab = [mm[i][:, :CHUNK] for i in n]
        tinv = [eye_f - jnp.where(level_masks[0], m_ab[i], 0.0) for i in n]
        for lm in level_masks[1:]:
            right = [_dot(jnp.where(lm, m_ab[i], 0.0), tinv[i]) for i in n]
            tinv = [tinv[i] - _dot(tinv[i], right[i]) for i in n]
        zero_top = jnp.zeros((CHUNK, 2 * hd), BF16)
        zmat = [khz[i].astype(F32) + _dot(mm[i], jnp.concatenate([zero_top, vz[i]], axis=0)) for i in n]
        xmat = [_dot(tinv[i], zmat[i]) for i in n]
        rhs = [jnp.concatenate([(-xmat[i]).astype(BF16), vz[i]], axis=0) for i in n]
        for i, (h, c) in enumerate(chains):
            idx = h * n_chunks + c
            ry = rhz[i].astype(F32) + _dot(aa[i], rhs[i])
            pq = jnp.where(eye_wide, g_row[i], 0.0) + _dot_tn(jnp.concatenate([bb[i], kb[i]], axis=0), rhs[i])
            rp_s[idx, 0:CHUNK, :] = ry.astype(BF16)
            rp_s[idx, CHUNK:CHUNK + hd, :] = pq.astype(BF16)

    for c0 in range(0, n_chunks, SCAN_GROUP_CHUNKS):
        build_group([(h, c) for c in range(c0, c0 + SCAN_GROUP_CHUNKS) for h in range(heads)])

    def chain(order):
        state = [h_s[h] for h in range(heads)]
        for c in order:
            out = [jnp.dot(rp_s[h * n_chunks + c], jnp.concatenate([state[h].astype(BF16), eye_b], axis=0),
                           preferred_element_type=F32) for h in range(heads)]
            state = [o[CHUNK:] for o in out]
            for h in range(heads):
                y_ref[0, 0, c * CHUNK:(c + 1) * CHUNK, h * hd:(h + 1) * hd] = out[h][:CHUNK]
        for h in range(heads):
            h_s[h] = state[h]

    @pl.when(fwd)
    def _():
        chain(range(n_chunks))

    @pl.when(jnp.logical_not(fwd))
    def _():
        chain(range(n_chunks - 1, -1, -1))


def _rwkv_scan(rwp, cw, k_k, k_a, r_k, w0, w2p, a0, a2p, t_lat, rw, heads):
    b, t_all, n_rw = rwp.shape
    n_tiles = t_all // TOK_TILE
    n_lat = t_lat // TOK_TILE
    hd = rw // heads
    halo = TOK_TILE // SUBLANE
    n_halo = t_all // SUBLANE

    def tile_of(d, s):
        return jnp.where(d == 0, s, jnp.where(s == 0, 0, n_tiles - s))

    def out_block(d, s):
        return jnp.clip(tile_of(d, s) - 1 + jnp.where((d == 1) & (s == 0), n_lat, 0), 0, n_lat - 1)

    out_spec = pl.BlockSpec((1, 1, TOK_TILE, rw), lambda i, d, s: (d, i, out_block(d, s), 0))
    dir_vec = pl.BlockSpec((1, 1, rw), lambda i, d, s: (d, 0, 0))
    dir_mat = pl.BlockSpec((1, LANE, rw), lambda i, d, s: (d, 0, 0))
    n_chunks = TOK_TILE // CHUNK
    narrow = pltpu.VMEM((heads, TOK_TILE, hd), BF16)
    wide = pltpu.VMEM((heads, TOK_TILE, 2 * hd), BF16)
    return pl.pallas_call(
        functools.partial(_scan_kernel, n_tiles=n_tiles, rw=rw, heads=heads),
        grid=(b, 2, n_tiles),
        in_specs=[pl.BlockSpec((1, TOK_TILE, n_rw), lambda i, d, s: (i, tile_of(d, s), 0)),
                  pl.BlockSpec((1, SUBLANE, n_rw),
                               lambda i, d, s: (i, jnp.maximum(tile_of(d, s) * halo - 1, 0), 0)),
                  pl.BlockSpec((1, SUBLANE, n_rw),
                               lambda i, d, s: (i, jnp.minimum((tile_of(d, s) + 1) * halo, n_halo - 1), 0)),
                  _const_spec(cw.shape), _const_spec((1, rw)), _const_spec((1, rw)), _const_spec((1, rw)),
                  dir_vec, dir_mat, dir_vec, dir_mat],
        out_specs=[out_spec, out_spec],
        out_shape=[jax.ShapeDtypeStruct((2, b, t_lat, rw), F32),
                   jax.ShapeDtypeStruct((2, b, t_lat, rw), F32)],
        scratch_shapes=[pltpu.VMEM((heads, hd, hd), F32),
                        wide, wide, wide,
                        pltpu.VMEM((heads, n_chunks * SUBLANE, 2 * hd), F32),
                        narrow, narrow, narrow, narrow,
                        pltpu.VMEM((heads * n_chunks, CHUNK + hd, 2 * hd), BF16)],
        compiler_params=_params("parallel", "arbitrary", "arbitrary"),
        name="rwkv_scan",
    )(rwp, rwp, rwp, cw, k_k, k_a, r_k, w0, w2p, a0, a2p)


def _mix_kernel(yf_ref, yb_ref, bf_ref, bb_ref, gl_ref, glp_ref, gln_ref, cwg_ref, g2_ref,
                lnw_ref, lnb_ref, yd_ref, wo_ref, gpost_ref, x_ref, gt_ref, gpre_ref, mod_ref,
                x1_ref, h2_ref, *, n_lat, rw, heads):
    i = pl.program_id(1)
    hd = rw // heads
    g_ones = _group_ones(rw, hd)
    y = yf_ref[0, 0] + yb_ref[0, 0]
    mu = _dot_hl(y, g_ones) * (1.0 / hd)
    yc = y - mu
    var = _dot_hl(yc * yc, g_ones) * (1.0 / hd)
    yn = yc * lax.rsqrt(var + LN_X_EPS) * lnw_ref[...] + lnb_ref[...]

    gl = gl_ref[0]
    prev_row = jnp.where(i == 0, 0.0, glp_ref[0][SUBLANE - 1:SUBLANE, :])
    next_row = jnp.where(i == n_lat - 1, 0.0, gln_ref[0][0:1, :])
    gm1, gp1 = _shift_rows(gl, prev_row, next_row)
    cw = cwg_ref[...]
    glc = cw[0:1] * gm1 + cw[1:2] * gl + cw[2:3] * gp1
    gate = _dot(_sigmoid(glc), g2_ref[...])

    yr = ((yn + bf_ref[0, 0] + bb_ref[0, 0]) * gate).astype(BF16)
    m = (jnp.dot(yr, wo_ref[0:rw, :], preferred_element_type=F32)
         + jnp.dot(yd_ref[0], wo_ref[rw:, :], preferred_element_type=F32))
    x1 = x_ref[0] + gt_ref[0] * _rms(m, gpost_ref[...], NORM_EPS)
    x1_ref[0] = x1
    mod = mod_ref[0]
    h2_ref[0] = (_rms(x1, gpre_ref[...], NORM_EPS) * (1.0 + mod[1:2]) + mod[0:1]).astype(BF16)


def _mix_out(y, bon, rwp, cwg, g2p, ln_w, ln_b, yd, w_out, g_post, x, gt1, g_pre2, mod2, rw, heads):
    b, t_lat, d = x.shape
    n_lat = t_lat // TOK_TILE
    off = (rwp.shape[1] - t_lat) // TOK_TILE
    gl_blk = (3 * rw + 2 * LANE) // LANE
    halo = TOK_TILE // SUBLANE
    dir_spec = lambda dd: pl.BlockSpec((1, 1, TOK_TILE, rw), lambda i, j: (dd, i, j, 0))
    tok = lambda w: pl.BlockSpec((1, TOK_TILE, w), lambda i, j: (i, j, 0))
    return pl.pallas_call(
        functools.partial(_mix_kernel, n_lat=n_lat, rw=rw, heads=heads),
        grid=(b, n_lat),
        in_specs=[dir_spec(0), dir_spec(1), dir_spec(0), dir_spec(1),
                  pl.BlockSpec((1, TOK_TILE, LANE), lambda i, j: (i, j + off, gl_blk)),
                  pl.BlockSpec((1, SUBLANE, LANE), lambda i, j: (i, (j + off) * halo - 1, gl_blk)),
                  pl.BlockSpec((1, SUBLANE, LANE),
                               lambda i, j: (i, jnp.minimum((j + off + 1) * halo, (n_lat + off) * halo - 1), gl_blk)),
                  _const_spec((3, LANE)), _const_spec((LANE, rw)),
                  _const_spec((1, rw)), _const_spec((1, rw)),
                  tok(d - rw), _const_spec((d, d)), _const_spec((1, d)),
                  tok(d),
                  pl.BlockSpec((1, 1, d), lambda i, j: (i, 0, 0)),
                  _const_spec((1, d)),
                  pl.BlockSpec((1, 2, d), lambda i, j: (i, 0, 0))],
        out_specs=[tok(d), tok(d)],
        out_shape=[jax.ShapeDtypeStruct((b, t_lat, d), F32),
                   jax.ShapeDtypeStruct((b, t_lat, d), BF16)],
        compiler_params=_params("parallel", "arbitrary"),
        name="mix_out",
    )(y, y, bon, bon, rwp, rwp, rwp, cwg, g2p, ln_w, ln_b, yd, w_out, g_post, x, gt1, g_pre2, mod2)


def _ffn_kernel(h_ref, hp_ref, hn_ref, wu_ref, cw_ref, cb_ref, wd_ref, x1_ref, gt_ref, gpost_ref,
                o_ref, *, n_tiles, d_ff):
    j = pl.program_id(1)
    first = j == 0
    last = j == n_tiles - 1
    lhs = jnp.concatenate([h_ref[0], hp_ref[0], hn_ref[0]], axis=0)
    prev_at = FFN_TILE + FFN_HALO - 1
    next_at = FFN_TILE + FFN_HALO
    starts = list(range(0, d_ff, FFN_COLS))

    def up(c0):
        return [jnp.dot(lhs, wu_ref[:, base:base + FFN_COLS], preferred_element_type=F32)
                for base in (c0, d_ff + c0)]

    def conv(u, base):
        cols = slice(base, base + FFN_COLS)
        main = u[:FFN_TILE]
        um1, up1 = _shift_rows(main, jnp.where(first, 0.0, u[prev_at:prev_at + 1]),
                               jnp.where(last, 0.0, u[next_at:next_at + 1]))
        cw = cw_ref[:, cols]
        return cw[0:1] * um1 + cw[1:2] * main + cw[2:3] * up1 + cb_ref[:, cols]

    acc = jnp.zeros(o_ref.shape[1:], F32)
    raw = up(starts[0])
    for i, c0 in enumerate(starts):
        nxt = up(starts[i + 1]) if i + 1 < len(starts) else None
        val = conv(raw[0], c0)
        gate = conv(raw[1], d_ff + c0)
        act = (val * gate * _sigmoid(gate)).astype(BF16)
        acc = acc + jnp.dot(act, wd_ref[c0:c0 + FFN_COLS, :], preferred_element_type=F32)
        raw = nxt
    o_ref[0] = x1_ref[0] + gt_ref[0] * _rms(acc, gpost_ref[...], NORM_EPS)


def _ffn(h2, w_up, conv_w, conv_b, w_down, x1, gt2, g_post):
    b, t_lat, d = x1.shape
    d_ff = w_down.shape[0]
    n_tiles = t_lat // FFN_TILE
    halo = FFN_TILE // FFN_HALO
    n_halo = t_lat // FFN_HALO
    tok = pl.BlockSpec((1, FFN_TILE, d), lambda i, j: (i, j, 0))
    return pl.pallas_call(
        functools.partial(_ffn_kernel, n_tiles=n_tiles, d_ff=d_ff),
        grid=(b, n_tiles),
        in_specs=[tok,
                  pl.BlockSpec((1, FFN_HALO, d), lambda i, j: (i, jnp.maximum(j * halo - 1, 0), 0)),
                  pl.BlockSpec((1, FFN_HALO, d), lambda i, j: (i, jnp.minimum((j + 1) * halo, n_halo - 1), 0)),
                  _const_spec(w_up.shape), _const_spec(conv_w.shape), _const_spec(conv_b.shape),
                  _const_spec(w_down.shape),
                  tok,
                  pl.BlockSpec((1, 1, d), lambda i, j: (i, 0, 0)),
                  _const_spec((1, d))],
        out_specs=tok,
        out_shape=jax.ShapeDtypeStruct((b, t_lat, d), F32),
        compiler_params=_params("parallel", "arbitrary"),
        name="conv_ffn",
    )(h2, h2, h2, w_up, conv_w, conv_b, w_down, x1, gt2, g_post)


def _rope_tables(t_ctx, t_lat, d_head):
    n_pair = d_head // 4
    pos = jnp.arange(t_lat, dtype=F32)
    row = jnp.floor(pos / GRID_W)
    col = pos - row * GRID_W
    inv = ROPE_THETA ** (-jnp.arange(n_pair, dtype=F32) / n_pair)
    ang = jnp.concatenate([row[:, None] * inv, col[:, None] * inv], axis=-1)
    ang = jnp.concatenate([jnp.zeros((t_ctx, d_head // 2), F32), ang], axis=0)
    cos, sin = jnp.cos(ang), jnp.sin(ang)
    zero = jnp.zeros_like(sin)
    reps = LANE // d_head
    tile = lambda a, b_: jnp.tile(jnp.concatenate([a, b_], axis=-1), (1, reps))
    return tile(cos, cos), tile(-sin, zero), tile(zero, sin)


def _pad_cols(a, width):
    return jnp.pad(a, ((0, 0), (0, width - a.shape[1])))


def _pad_rows(a, height):
    return jnp.pad(a, ((0, height - a.shape[0]), (0, 0)))


def kernel(x, c, ctx, c_ctx, w_mod, b_mod, g_pre_mix, g_post_mix, g_pre_ffn, g_post_ffn, w_in, rwkv_conv, w0_fwd, w2_fwd, a0_fwd, a2_fwd, w0_bwd, w2_bwd, a0_bwd, a2_bwd, g2, k_k, k_a, r_k, ln_x_w, ln_x_b, lam_q1, lam_k1, lam_q2, lam_k2, subln_w, w_out, w_up, ffn_conv, ffn_conv_b, w_down):
    depth = w_in.shape[0]
    assert depth == 1, "single-layer block only"
    b, t_lat, d = x.shape
    t_ctx = ctx.shape[1]
    assert t_ctx == TOK_TILE and t_lat % FFN_TILE == 0 and t_lat % GRID_W == 0
    rw = k_k.shape[-1]
    heads = r_k.shape[1]
    d_head = lam_q1.shape[-1]
    width = d - rw
    n_lora = (w2_fwd.shape[1], a2_fwd.shape[1], g2.shape[1])
    assert max(n_lora) <= LANE
    lam_init = 0.8 - 0.6 * math.exp(-0.3 * 0)

    cc = _pad_rows(jnp.concatenate([c, c_ctx[None, :]], axis=0), 2 * SUBLANE)
    mod = _modulation(cc, w_mod[0], b_mod)
    sh1, sc1, gt1, sh2, sc2, gt2 = jnp.split(mod[:b], 6, axis=-1)
    sh1c, sc1c = mod[b, 0:d], mod[b, d:2 * d]
    mod1 = jnp.stack([jnp.stack([jnp.broadcast_to(sh1c, (b, d)), jnp.broadcast_to(sc1c, (b, d))], axis=1),
                      jnp.stack([sh1, sc1], axis=1)], axis=1)
    mod2 = jnp.stack([sh2, sc2], axis=1)

    w = w_in[0]
    o = 3 * rw
    cuts = (o, o + n_lora[0], o + n_lora[0] + n_lora[1], o + sum(n_lora))
    n_rw = o + 3 * LANE
    w_p = jnp.concatenate([w[:, :o], _pad_cols(w[:, cuts[0]:cuts[1]], LANE), _pad_cols(w[:, cuts[1]:cuts[2]], LANE),
                           _pad_cols(w[:, cuts[2]:cuts[3]], LANE), w[:, cuts[3]:]], axis=1).astype(BF16)
    cv = rwkv_conv[0]
    cw = jnp.concatenate([cv[:, :o], _pad_cols(cv[:, cuts[0]:cuts[1]], LANE), _pad_cols(cv[:, cuts[1]:cuts[2]], LANE),
                          _pad_cols(cv[:, cuts[2]:cuts[3]], LANE)], axis=1)
    rope_c, rope_s1, rope_s2 = _rope_tables(t_ctx, t_lat, d_head)

    rwp, q, k, v = _inproj(x, ctx, mod1, g_pre_mix, w_p, rope_c, rope_s1, rope_s2, n_rw, width,
                           float(d_head) ** -0.5 * math.log2(math.e))

    lam_p = jnp.concatenate([lam_q1, lam_k1, lam_q2, lam_k2], axis=0)
    yd = _attention(q, k, v, lam_p, subln_w, t_lat, lam_init)

    w0 = jnp.stack([w0_fwd, w0_bwd], axis=0)
    a0 = jnp.stack([a0_fwd, a0_bwd], axis=0)
    w2p = jnp.stack([_pad_rows(w2_fwd[0], LANE), _pad_rows(w2_bwd[0], LANE)], axis=0).astype(BF16)
    a2p = jnp.stack([_pad_rows(a2_fwd[0], LANE), _pad_rows(a2_bwd[0], LANE)], axis=0).astype(BF16)
    y, bon = _rwkv_scan(rwp, cw, k_k, k_a, r_k.reshape(1, rw), w0, w2p, a0, a2p, t_lat, rw, heads)

    x1, h2 = _mix_out(y, bon, rwp, cw[:, o + 2 * LANE:], _pad_rows(g2[0], LANE).astype(BF16), ln_x_w, ln_x_b,
                      yd, w_out[0].astype(BF16), g_post_mix, x, gt1[:, None, :], g_pre_ffn, mod2, rw, heads)

    return _ffn(h2, w_up[0].astype(BF16), ffn_conv[0], ffn_conv_b, w_down[0].astype(BF16), x1,
                gt2[:, None, :], g_post_ffn)
```

```python
import functools
import math

import jax
import jax.numpy as jnp
from jax import lax
from jax.experimental import pallas as pl
from jax.experimental.pallas import tpu as pltpu

F32 = jnp.float32
BF16 = jnp.bfloat16

LANE = 128
SUBLANE = 8
MXU_WIDTH = 256
TOK_TILE = 256
CHUNK = 64
ATTN_TILE = 512
ATTN_SUB = 256
FFN_TILE = 1024
FFN_HALO = 16
FFN_COLS = 256
FFN_DOWN_GROUP = 11
VMEM_LIMIT = 56 * 1024 * 1024

GRID_W = 64
ROPE_THETA = 10000.0
NORM_EPS = 1e-6
LN_X_EPS = 64e-5
SUBLN_EPS = 1e-5

_NT = (((1,), (1,)), ((), ()))
_TN = (((0,), (0,)), ((), ()))


def _dot(a, b):
    return jnp.dot(a.astype(BF16), b.astype(BF16), preferred_element_type=F32)


def _dot_nt(a, b):
    return lax.dot_general(a.astype(BF16), b.astype(BF16), _NT, preferred_element_type=F32)


def _dot_tn(a, b):
    return lax.dot_general(a.astype(BF16), b.astype(BF16), _TN, preferred_element_type=F32)


def _split(a):
    hi = a.astype(BF16)
    lo = (a - hi.astype(F32)).astype(BF16)
    return hi, lo


def _dot_hl(a, b_exact):
    hi, lo = _split(a)
    return (jnp.dot(hi, b_exact, preferred_element_type=F32)
            + jnp.dot(lo, b_exact, preferred_element_type=F32))


def _dot_lh(a_exact, b):
    hi, lo = _split(b)
    return (jnp.dot(a_exact, hi, preferred_element_type=F32)
            + jnp.dot(a_exact, lo, preferred_element_type=F32))


def _dot3(a, b):
    ah, al = _split(a)
    bh, bl = _split(b)
    return (jnp.dot(ah, bh, preferred_element_type=F32)
            + jnp.dot(ah, bl, preferred_element_type=F32)
            + jnp.dot(al, bh, preferred_element_type=F32))


def _sigmoid(z):
    return 1.0 / (1.0 + jnp.exp(-z))


def _rms(x, g, eps):
    return x * lax.rsqrt(jnp.mean(x * x, axis=-1, keepdims=True) + eps) * g


def _group_ones(width, group):
    i = lax.broadcasted_iota(jnp.int32, (width, width), 0) // group
    j = lax.broadcasted_iota(jnp.int32, (width, width), 1) // group
    return jnp.where(i == j, 1.0, 0.0).astype(BF16)


def _group_sum(x, group, exact=False):
    ones = _group_ones(MXU_WIDTH, group)
    pieces = []
    for c0 in range(0, x.shape[1], MXU_WIDTH):
        piece = x[:, c0:c0 + MXU_WIDTH]
        pieces.append(_dot_hl(piece, ones) if exact else _dot(piece, ones))
    return jnp.concatenate(pieces, axis=1)


def _shift_rows(x, prev_row, next_row):
    n = x.shape[0]
    row = lax.broadcasted_iota(jnp.int32, (n, 1), 0)
    xm1 = jnp.where(row == 0, prev_row, pltpu.roll(x, 1, axis=0))
    xp1 = jnp.where(row == n - 1, next_row, pltpu.roll(x, n - 1, axis=0))
    return xm1, xp1


def _params(*sem):
    return pltpu.CompilerParams(dimension_semantics=sem, vmem_limit_bytes=VMEM_LIMIT)


def _const_spec(shape):
    n = len(shape)
    return pl.BlockSpec(shape, lambda *_: (0,) * n, pipeline_mode=pl.Buffered(1))


def _mod_kernel(c_ref, w_ref, b_ref, o_ref):
    c = c_ref[...]
    o_ref[...] = _dot3(c * _sigmoid(c), w_ref[...]) + b_ref[...]


def _modulation(cc, w_mod, b_mod):
    rows, d = cc.shape
    n = w_mod.shape[1]
    tn = 1024
    return pl.pallas_call(
        _mod_kernel,
        grid=(n // tn,),
        in_specs=[pl.BlockSpec((rows, d), lambda j: (0, 0)),
                  pl.BlockSpec((d, tn), lambda j: (0, j)),
                  pl.BlockSpec((1, tn), lambda j: (0, j))],
        out_specs=pl.BlockSpec((rows, tn), lambda j: (0, j)),
        out_shape=jax.ShapeDtypeStruct((rows, n), F32),
        compiler_params=_params("parallel"),
        name="adaln_mod",
    )(cc, w_mod, b_mod)


def _inproj_kernel(x_ref, ctx_ref, mod_ref, g_ref, w_ref, rc_ref, rs1_ref, rs2_ref,
                   rw_ref, q_ref, k_ref, v_ref, *, n_rw, width, scale):
    t = pl.program_id(1)
    xin = jnp.where(t == 0, ctx_ref[0], x_ref[0])
    mod = mod_ref[0, 0]
    h = (_rms(xin, g_ref[...], NORM_EPS) * (1.0 + mod[1:2]) + mod[0:1]).astype(BF16)

    step = 4 * LANE
    for c0 in range(0, n_rw, step):
        c1 = min(c0 + step, n_rw)
        rw_ref[0, :, c0:c1] = jnp.dot(h, w_ref[:, c0:c1], preferred_element_type=F32)

    reps = width // LANE
    cos = jnp.concatenate([rc_ref[...]] * reps, axis=1)
    s1 = jnp.concatenate([rs1_ref[...]] * reps, axis=1)
    s2 = jnp.concatenate([rs2_ref[...]] * reps, axis=1)
    half = GRID_W // 2

    def rope(z):
        return z * cos + pltpu.roll(z, width - half, axis=1) * s1 + pltpu.roll(z, half, axis=1) * s2

    q = jnp.dot(h, w_ref[:, n_rw:n_rw + width], preferred_element_type=F32)
    q_ref[0] = (rope(q) * scale).astype(BF16)
    k = jnp.dot(h, w_ref[:, n_rw + width:n_rw + 2 * width], preferred_element_type=F32)
    k_ref[0] = rope(k).astype(BF16)
    v = jnp.dot(h, w_ref[:, n_rw + 2 * width:n_rw + 3 * width], preferred_element_type=F32)
    v_ref[0] = v.astype(BF16)


def _inproj(x, ctx, mod1, g_pre, w_p, rope_c, rope_s1, rope_s2, n_rw, width, scale):
    b, t_lat, d = x.shape
    n_ctx = ctx.shape[1] // TOK_TILE
    n_tiles = t_lat // TOK_TILE + n_ctx
    t_all = n_tiles * TOK_TILE
    tok = lambda w: pl.BlockSpec((1, TOK_TILE, w), lambda i, t: (i, t, 0))
    return pl.pallas_call(
        functools.partial(_inproj_kernel, n_rw=n_rw, width=width, scale=scale),
        grid=(b, n_tiles),
        in_specs=[pl.BlockSpec((1, TOK_TILE, d), lambda i, t: (i, jnp.maximum(t - 1, 0), 0)),
                  pl.BlockSpec((1, TOK_TILE, d), lambda i, t: (i, 0, 0)),
                  pl.BlockSpec((1, 1, 2, d), lambda i, t: (i, jnp.minimum(t, 1), 0, 0)),
                  _const_spec((1, d)),
                  _const_spec(w_p.shape),
                  pl.BlockSpec((TOK_TILE, LANE), lambda i, t: (t, 0)),
                  pl.BlockSpec((TOK_TILE, LANE), lambda i, t: (t, 0)),
                  pl.BlockSpec((TOK_TILE, LANE), lambda i, t: (t, 0))],
        out_specs=[tok(n_rw),
                   pl.BlockSpec((1, TOK_TILE, width), lambda i, t: (i, jnp.maximum(t - n_ctx, 0), 0)),
                   tok(width), tok(width)],
        out_shape=[jax.ShapeDtypeStruct((b, t_all, n_rw), F32),
                   jax.ShapeDtypeStruct((b, t_lat, width), BF16),
                   jax.ShapeDtypeStruct((b, t_all, width), BF16),
                   jax.ShapeDtypeStruct((b, t_all, width), BF16)],
        compiler_params=_params("parallel", "arbitrary"),
        name="in_proj",
    )(x, ctx, mod1, g_pre, w_p, rope_c, rope_s1, rope_s2)


def _attn_kernel(q_ref, k_ref, v_ref, lam_ref, g_ref, o_ref, *, lam_init):
    lp = lam_ref[...]
    lam = (jnp.exp(jnp.sum(lp[0:1] * lp[1:2], axis=-1, keepdims=True))
           - jnp.exp(jnp.sum(lp[2:3] * lp[3:4], axis=-1, keepdims=True)) + lam_init)
    k = k_ref[0]
    d = k.shape[-1] // 2
    lane = lax.broadcasted_iota(jnp.int32, (ATTN_SUB, 2 * d), 1)
    zero = jnp.zeros((ATTN_SUB, 2 * d), BF16)

    scores = []
    for i in range(ATTN_TILE // ATTN_SUB):
        q = q_ref[0, i * ATTN_SUB:(i + 1) * ATTN_SUB, :]
        scores.append([lax.dot_general(jnp.where(sel, q, zero), k, _NT, preferred_element_type=F32)
                       for sel in (lane < d, lane >= d)])
    for i, (s1, s2) in enumerate(scores):
        p1 = jnp.exp2(s1 - jnp.max(s1, axis=-1, keepdims=True))
        l1 = jnp.sum(p1, axis=-1, keepdims=True)
        p2 = jnp.exp2(s2 - jnp.max(s2, axis=-1, keepdims=True))
        l2 = jnp.sum(p2, axis=-1, keepdims=True)
        coef = (lam * l1 / l2).astype(BF16)
        a = p1.astype(BF16) - coef * p2.astype(BF16)
        o = jnp.dot(a, v_ref[0], preferred_element_type=F32) * (1.0 / l1)
        o_ref[0, i * ATTN_SUB:(i + 1) * ATTN_SUB, :] = (
            _rms(o, g_ref[...], SUBLN_EPS) * (1.0 - lam_init)).astype(BF16)


def _attention(q, k, v, lam_p, subln_w, t_lat, lam_init):
    b, t_all, width = k.shape
    hd = subln_w.shape[-1]
    heads = width // hd
    return pl.pallas_call(
        functools.partial(_attn_kernel, lam_init=lam_init),
        grid=(b, heads, t_lat // ATTN_TILE),
        in_specs=[pl.BlockSpec((1, ATTN_TILE, hd), lambda i, h, j: (i, j, h)),
                  pl.BlockSpec((1, t_all, hd), lambda i, h, j: (i, 0, h)),
                  pl.BlockSpec((1, t_all, hd), lambda i, h, j: (i, 0, h)),
                  _const_spec(lam_p.shape),
                  _const_spec((1, hd))],
        out_specs=pl.BlockSpec((1, ATTN_TILE, hd), lambda i, h, j: (i, j, h)),
        out_shape=jax.ShapeDtypeStruct((b, t_lat, width), BF16),
        compiler_params=_params("parallel", "parallel", "arbitrary"),
        name="diff_attn",
    )(q, k, v, lam_p, subln_w)


def _build_kernel(main_ref, prev_ref, next_ref, cw_ref, kk_ref, ka_ref, rk_ref,
                  w0_ref, w2_ref, a0_ref, a2_ref, g2_ref,
                  rp_ref, bon_ref, gate_ref,
                  khz_s, rhz_s, vz_s, gamz_s, kt_s, bt_s, kb_s, bb_s, *, n_tiles, rw, heads):
    tile = pl.program_id(1)
    hd = rw // heads
    n_chunks = TOK_TILE // CHUNK

    x = main_ref[0]
    starts = (tile == 0) | (tile == 1)
    ends = (tile == 0) | (tile == n_tiles - 1)
    prev_row = jnp.where(starts, 0.0, prev_ref[0][SUBLANE - 1:SUBLANE, :])
    next_row = jnp.where(ends, 0.0, next_ref[0][0:1, :])
    xm1, xp1 = _shift_rows(x, prev_row, next_row)
    cw = cw_ref[...]
    xc = cw[0:1] * xm1 + cw[1:2] * x + cw[2:3] * xp1

    r = xc[:, 0:rw]
    k = xc[:, rw:2 * rw]
    v = xc[:, 2 * rw:3 * rw]
    wl = xc[:, 3 * rw:3 * rw + LANE]
    al = xc[:, 3 * rw + LANE:3 * rw + 2 * LANE]
    gl = xc[:, 3 * rw + 2 * LANE:3 * rw + 3 * LANE]
    gate_ref[0] = _dot(_sigmoid(gl), g2_ref[...])

    kk = k * kk_ref[...]
    kk = kk / jnp.maximum(jnp.sqrt(_group_sum(kk * kk, hd)), 1e-12)
    w_lat = jnp.tanh(wl).astype(BF16)
    al_b = al.astype(BF16)

    ri = lax.broadcasted_iota(jnp.int32, (TOK_TILE, TOK_TILE), 0)
    ci = lax.broadcasted_iota(jnp.int32, (TOK_TILE, TOK_TILE), 1)
    same = (ri // CHUNK) == (ci // CHUNK)
    bonus = jnp.zeros((TOK_TILE, rw), F32)
    for d in range(2):
        z = w0_ref[d] + jnp.dot(w_lat, w2_ref[d], preferred_element_type=F32)
        w = -(jnp.maximum(-z, 0.0) + jnp.log(1.0 + jnp.exp(-jnp.abs(z)))) - 0.5
        ld = -jnp.exp(w)
        a = _sigmoid(a0_ref[d] + jnp.dot(al_b, a2_ref[d], preferred_element_type=F32))
        kd = k * (1.0 + (a - 1.0) * ka_ref[...])
        bvec = kk * a
        bonus = bonus + 0.5 * _group_sum(r * kd * rk_ref[...], hd) * v

        before = (ci <= ri) if d == 0 else (ci >= ri)
        cum = _dot_lh(jnp.where(same & before, 1.0, 0.0).astype(BF16), ld)
        last = [c * CHUNK + CHUNK - 1 if d == 0 else c * CHUNK for c in range(n_chunks)]
        ctot = jnp.concatenate([jnp.broadcast_to(cum[i:i + 1], (CHUNK, rw)) for i in last], axis=0)
        e_neg = jnp.exp(-cum)
        e_rem = jnp.exp(ctot - cum)
        gam = jnp.exp(jnp.concatenate([ctot[c * CHUNK:c * CHUNK + SUBLANE] for c in range(n_chunks)], axis=0))
        narrow = ((kt_s, kd * e_neg), (bt_s, bvec * e_neg), (kb_s, kd * e_rem), (bb_s, bvec * e_rem))
        wide_lo = ((khz_s, kk * jnp.exp(cum - ld)), (rhz_s, r * jnp.exp(cum)), (gamz_s, gam))
        for h in range(heads):
            sl = slice(h * hd, (h + 1) * hd)
            for ref, val in narrow:
                ref[d * heads + h] = val[:, sl].astype(BF16)
            for ref, val in wide_lo:
                lo = val[:, sl]
                ref[d * heads + h] = jnp.concatenate([lo, jnp.zeros_like(lo)], axis=1).astype(ref.dtype)
            if d == 0:
                vz_s[h] = jnp.concatenate([jnp.zeros((TOK_TILE, hd), F32), v[:, sl]], axis=1).astype(BF16)
    bon_ref[0] = bonus

    ii = lax.broadcasted_iota(jnp.int32, (CHUNK, CHUNK), 0)
    jj = lax.broadcasted_iota(jnp.int32, (CHUNK, CHUNK), 1)
    eye_f = jnp.where(ii == jj, 1.0, 0.0)
    level_masks = []
    size = 1
    while size < CHUNK:
        level_masks.append(((ii // (2 * size)) == (jj // (2 * size))) & ((ii // size) != (jj // size)))
        size *= 2
    iw = lax.broadcasted_iota(jnp.int32, (CHUNK, 2 * CHUNK), 0)
    jw = lax.broadcasted_iota(jnp.int32, (CHUNK, 2 * CHUNK), 1)
    jw = jnp.where(jw >= CHUNK, jw - CHUNK, jw)
    strict = (jw < iw, jw > iw)
    incl = (jw <= iw, jw >= iw)
    eye_wide = (lax.broadcasted_iota(jnp.int32, (hd, 2 * hd), 0)
                == lax.broadcasted_iota(jnp.int32, (hd, 2 * hd), 1))
    zero_top = jnp.zeros((CHUNK, 2 * hd), BF16)

    chains = [(d, h, c) for d in range(2) for c in range(n_chunks) for h in range(heads)]
    n = range(len(chains))
    rows = [slice(c * CHUNK, (c + 1) * CHUNK) for _, _, c in chains]
    ld_ = lambda ref: [ref[d * heads + h, rows[i], :] for i, (d, h, _) in enumerate(chains)]
    khz, rhz, kt, bt, kb, bb = (ld_(ref) for ref in (khz_s, rhz_s, kt_s, bt_s, kb_s, bb_s))
    vz = [vz_s[h, rows[i], :] for i, (_, h, _) in enumerate(chains)]
    g_row = [gamz_s[d * heads + h, c * SUBLANE:c * SUBLANE + 1, :] for d, h, c in chains]
    gram = [_dot_nt(jnp.concatenate([khz[i][:, :hd], rhz[i][:, :hd]], axis=0),
                    jnp.concatenate([bt[i], kt[i]], axis=0)) for i in n]
    mm = [jnp.where(strict[chains[i][0]], gram[i][:CHUNK], 0.0) for i in n]
    aa = [jnp.where(incl[chains[i][0]], gram[i][CHUNK:], 0.0) for i in n]
    m_ab = [mm[i][:, :CHUNK] for i in n]
    tinv = [eye_f - jnp.where(level_masks[0], m_ab[i], 0.0) for i in n]
    for lm in level_masks[1:]:
        right = [_dot(jnp.where(lm, m_ab[i], 0.0), tinv[i]) for i in n]
        tinv = [tinv[i] - _dot(tinv[i], right[i]) for i in n]
    zmat = [khz[i].astype(F32) + _dot(mm[i], jnp.concatenate([zero_top, vz[i]], axis=0)) for i in n]
    xmat = [_dot(tinv[i], zmat[i]) for i in n]
    rhs = [jnp.concatenate([(-xmat[i]).astype(BF16), vz[i]], axis=0) for i in n]
    for i, (d, h, c) in enumerate(chains):
        idx = (d * heads + h) * n_chunks + c
        ry = rhz[i].astype(F32) + _dot(aa[i], rhs[i])
        pq = jnp.where(eye_wide, g_row[i], 0.0) + _dot_tn(jnp.concatenate([bb[i], kb[i]], axis=0), rhs[i])
        rp_ref[0, 0, idx, 0:CHUNK, :] = ry.astype(BF16)
        rp_ref[0, 0, idx, CHUNK:CHUNK + hd, :] = pq.astype(BF16)


def _rwkv_build(rwp, cw, k_k, k_a, r_k, w0, w2p, a0, a2p, g2p, t_lat, rw, heads):
    b, t_all, n_rw = rwp.shape
    n_tiles = t_all // TOK_TILE
    n_ctx = n_tiles - t_lat // TOK_TILE
    hd = rw // heads
    halo = TOK_TILE // SUBLANE
    n_halo = t_all // SUBLANE
    n_chunks = TOK_TILE // CHUNK
    n_mat = 2 * heads * n_chunks
    narrow = pltpu.VMEM((2 * heads, TOK_TILE, hd), BF16)
    wide = pltpu.VMEM((2 * heads, TOK_TILE, 2 * hd), BF16)
    lat = pl.BlockSpec((1, TOK_TILE, rw), lambda i, t: (i, jnp.maximum(t - n_ctx, 0), 0))
    return pl.pallas_call(
        functools.partial(_build_kernel, n_tiles=n_tiles, rw=rw, heads=heads),
        grid=(b, n_tiles),
        in_specs=[pl.BlockSpec((1, TOK_TILE, n_rw), lambda i, t: (i, t, 0)),
                  pl.BlockSpec((1, SUBLANE, n_rw), lambda i, t: (i, jnp.maximum(t * halo - 1, 0), 0)),
                  pl.BlockSpec((1, SUBLANE, n_rw), lambda i, t: (i, jnp.minimum((t + 1) * halo, n_halo - 1), 0)),
                  _const_spec(cw.shape), _const_spec((1, rw)), _const_spec((1, rw)), _const_spec((1, rw)),
                  _const_spec(w0.shape), _const_spec(w2p.shape), _const_spec(a0.shape), _const_spec(a2p.shape),
                  _const_spec(g2p.shape)],
        out_specs=[pl.BlockSpec((1, 1, n_mat, CHUNK + hd, 2 * hd), lambda i, t: (i, t, 0, 0, 0)), lat, lat],
        out_shape=[jax.ShapeDtypeStruct((b, n_tiles, n_mat, CHUNK + hd, 2 * hd), BF16),
                   jax.ShapeDtypeStruct((b, t_lat, rw), F32),
                   jax.ShapeDtypeStruct((b, t_lat, rw), F32)],
        scratch_shapes=[wide, wide, pltpu.VMEM((heads, TOK_TILE, 2 * hd), BF16),
                        pltpu.VMEM((2 * heads, n_chunks * SUBLANE, 2 * hd), F32),
                        narrow, narrow, narrow, narrow],
        compiler_params=_params("parallel", "arbitrary"),
        name="rwkv_build",
    )(rwp, rwp, rwp, cw, k_k, k_a, r_k, w0, w2p, a0, a2p, g2p)


def _chain_kernel(rpf_ref, rpb_ref, yf_ref, yb_ref, h_s, *, heads, hd):
    n_chunks = TOK_TILE // CHUNK

    @pl.when(pl.program_id(1) == 0)
    def _():
        h_s[...] = jnp.zeros_like(h_s)

    ii = lax.broadcasted_iota(jnp.int32, (hd, hd), 0)
    jj = lax.broadcasted_iota(jnp.int32, (hd, hd), 1)
    eye_b = jnp.where(ii == jj, 1.0, 0.0).astype(BF16)
    state = [h_s[i] for i in range(2 * heads)]
    for j in range(n_chunks):
        jobs = [(0, rpf_ref, yf_ref, j), (1, rpb_ref, yb_ref, n_chunks - 1 - j)]
        out = [[jnp.dot(ref[0, 0, h * n_chunks + c],
                        jnp.concatenate([state[d * heads + h].astype(BF16), eye_b], axis=0),
                        preferred_element_type=F32) for h in range(heads)] for d, ref, _, c in jobs]
        for d, _, y_ref, c in jobs:
            for h in range(heads):
                state[d * heads + h] = out[d][h][CHUNK:]
                y_ref[0, c * CHUNK:(c + 1) * CHUNK, h * hd:(h + 1) * hd] = out[d][h][:CHUNK]
    for i in range(2 * heads):
        h_s[i] = state[i]


def _rwkv_chain(rp, t_lat, rw, heads):
    b, n_tiles, n_mat = rp.shape[:3]
    n_lat = t_lat // TOK_TILE
    hd = rw // heads
    per_dir = n_mat // 2

    def bwd_tile(s):
        return jnp.where(s == 0, 0, n_tiles - s)

    mats = lambda d, tile: pl.BlockSpec((1, 1, per_dir, CHUNK + hd, 2 * hd),
                                        lambda i, s: (i, tile(s), d, 0, 0))
    yf_spec = pl.BlockSpec((1, TOK_TILE, rw), lambda i, s: (i, jnp.maximum(s - (n_tiles - n_lat), 0), 0))
    yb_spec = pl.BlockSpec((1, TOK_TILE, rw),
                           lambda i, s: (i, jnp.where(s == 0, n_lat - 1, bwd_tile(s) - (n_tiles - n_lat)), 0))
    return pl.pallas_call(
        functools.partial(_chain_kernel, heads=heads, hd=hd),
        grid=(b, n_tiles),
        in_specs=[mats(0, lambda s: s), mats(1, bwd_tile)],
        out_specs=[yf_spec, yb_spec],
        out_shape=[jax.ShapeDtypeStruct((b, t_lat, rw), F32), jax.ShapeDtypeStruct((b, t_lat, rw), F32)],
        scratch_shapes=[pltpu.VMEM((2 * heads, hd, hd), F32)],
        compiler_params=_params("parallel", "arbitrary"),
        name="rwkv_chain",
    )(rp, rp)


def _mix_kernel(yf_ref, yb_ref, bon_ref, gate_ref, lnw_ref, lnb_ref, yd_ref, wo_ref, gpost_ref,
                x_ref, gt_ref, gpre_ref, mod_ref, x1_ref, h2_ref, *, rw, heads):
    hd = rw // heads
    y = yf_ref[0] + yb_ref[0]
    mu = _group_sum(y, hd, exact=True) * (1.0 / hd)
    yc = y - mu
    var = _group_sum(yc * yc, hd, exact=True) * (1.0 / hd)
    yn = yc * lax.rsqrt(var + LN_X_EPS) * lnw_ref[...] + lnb_ref[...]
    yr = ((yn + bon_ref[0]) * gate_ref[0]).astype(BF16)
    m = (jnp.dot(yr, wo_ref[0:rw, :], preferred_element_type=F32)
         + jnp.dot(yd_ref[0], wo_ref[rw:, :], preferred_element_type=F32))
    x1 = x_ref[0] + gt_ref[0] * _rms(m, gpost_ref[...], NORM_EPS)
    x1_ref[0] = x1
    mod = mod_ref[0]
    h2_ref[0] = (_rms(x1, gpre_ref[...], NORM_EPS) * (1.0 + mod[1:2]) + mod[0:1]).astype(BF16)


def _mix_out(yf, yb, bon, gate, ln_w, ln_b, yd, w_out, g_post, x, gt1, g_pre2, mod2, rw, heads):
    b, t_lat, d = x.shape
    tok = lambda w: pl.BlockSpec((1, TOK_TILE, w), lambda i, j: (i, j, 0))
    return pl.pallas_call(
        functools.partial(_mix_kernel, rw=rw, heads=heads),
        grid=(b, t_lat // TOK_TILE),
        in_specs=[tok(rw), tok(rw), tok(rw), tok(rw),
                  _const_spec((1, rw)), _const_spec((1, rw)),
                  tok(d - rw), _const_spec((d, d)), _const_spec((1, d)),
                  tok(d),
                  pl.BlockSpec((1, 1, d), lambda i, j: (i, 0, 0)),
                  _const_spec((1, d)),
                  pl.BlockSpec((1, 2, d), lambda i, j: (i, 0, 0))],
        out_specs=[tok(d), tok(d)],
        out_shape=[jax.ShapeDtypeStruct((b, t_lat, d), F32),
                   jax.ShapeDtypeStruct((b, t_lat, d), BF16)],
        compiler_params=_params("parallel", "arbitrary"),
        name="mix_out",
    )(yf, yb, bon, gate, ln_w, ln_b, yd, w_out, g_post, x, gt1, g_pre2, mod2)


def _ffn_kernel(h_ref, hp_ref, hn_ref, wu_ref, cw_ref, cb_ref, wd_ref, x1_ref, gt_ref, gpost_ref,
                o_ref, act_s, *, n_tiles, d_ff):
    j = pl.program_id(1)
    first = j == 0
    last = j == n_tiles - 1
    lhs = jnp.concatenate([h_ref[0], hp_ref[0], hn_ref[0]], axis=0)
    prev_at = FFN_TILE + FFN_HALO - 1
    next_at = FFN_TILE + FFN_HALO
    starts = list(range(0, d_ff, FFN_COLS))

    def up(c0):
        return [jnp.dot(lhs, wu_ref[:, base:base + FFN_COLS], preferred_element_type=F32)
                for base in (c0, d_ff + c0)]

    def conv(u, base):
        cols = slice(base, base + FFN_COLS)
        main = u[:FFN_TILE]
        um1, up1 = _shift_rows(main, jnp.where(first, 0.0, u[prev_at:prev_at + 1]),
                               jnp.where(last, 0.0, u[next_at:next_at + 1]))
        cw = cw_ref[:, cols]
        return cw[0:1] * um1 + cw[1:2] * main + cw[2:3] * up1 + cb_ref[:, cols]

    acc = None
    raw = up(starts[0])
    for i, c0 in enumerate(starts):
        nxt = up(starts[i + 1]) if i + 1 < len(starts) else None
        val = conv(raw[0], c0)
        gate = conv(raw[1], d_ff + c0)
        act_s[:, c0:c0 + FFN_COLS] = (val * gate * _sigmoid(gate)).astype(BF16)
        raw = nxt
        if (i + 1) % FFN_DOWN_GROUP == 0 or i + 1 == len(starts):
            k0 = (i // FFN_DOWN_GROUP) * FFN_DOWN_GROUP * FFN_COLS
            part = jnp.dot(act_s[:, k0:c0 + FFN_COLS], wd_ref[k0:c0 + FFN_COLS, :], preferred_element_type=F32)
            acc = part if acc is None else acc + part
    o_ref[0] = x1_ref[0] + gt_ref[0] * _rms(acc, gpost_ref[...], NORM_EPS)


def _ffn(h2, w_up, conv_w, conv_b, w_down, x1, gt2, g_post):
    b, t_lat, d = x1.shape
    d_ff = w_down.shape[0]
    n_tiles = t_lat // FFN_TILE
    halo = FFN_TILE // FFN_HALO
    n_halo = t_lat // FFN_HALO
    tok = pl.BlockSpec((1, FFN_TILE, d), lambda i, j: (i, j, 0))
    return pl.pallas_call(
        functools.partial(_ffn_kernel, n_tiles=n_tiles, d_ff=d_ff),
        grid=(b, n_tiles),
        in_specs=[tok,
                  pl.BlockSpec((1, FFN_HALO, d), lambda i, j: (i, jnp.maximum(j * halo - 1, 0), 0)),
                  pl.BlockSpec((1, FFN_HALO, d), lambda i, j: (i, jnp.minimum((j + 1) * halo, n_halo - 1), 0)),
                  _const_spec(w_up.shape), _const_spec(conv_w.shape), _const_spec(conv_b.shape),
                  _const_spec(w_down.shape),
                  tok,
                  pl.BlockSpec((1, 1, d), lambda i, j: (i, 0, 0)),
                  _const_spec((1, d))],
        out_specs=tok,
        out_shape=jax.ShapeDtypeStruct((b, t_lat, d), F32),
        scratch_shapes=[pltpu.VMEM((FFN_TILE, d_ff), BF16)],
        compiler_params=_params("parallel", "arbitrary"),
        name="conv_ffn",
    )(h2, h2, h2, w_up, conv_w, conv_b, w_down, x1, gt2, g_post)


def _rope_tables(t_ctx, t_lat, d_head):
    n_pair = d_head // 4
    pos = jnp.arange(t_lat, dtype=F32)
    row = jnp.floor(pos / GRID_W)
    col = pos - row * GRID_W
    inv = ROPE_THETA ** (-jnp.arange(n_pair, dtype=F32) / n_pair)
    ang = jnp.concatenate([row[:, None] * inv, col[:, None] * inv], axis=-1)
    ang = jnp.concatenate([jnp.zeros((t_ctx, d_head // 2), F32), ang], axis=0)
    cos, sin = jnp.cos(ang), jnp.sin(ang)
    zero = jnp.zeros_like(sin)
    reps = LANE // d_head
    tile = lambda a, b_: jnp.tile(jnp.concatenate([a, b_], axis=-1), (1, reps))
    return tile(cos, cos), tile(-sin, zero), tile(zero, sin)


def _pad_cols(a, width):
    return jnp.pad(a, ((0, 0), (0, width - a.shape[1])))


def _pad_rows(a, height):
    return jnp.pad(a, ((0, height - a.shape[0]), (0, 0)))


def kernel(x, c, ctx, c_ctx, w_mod, b_mod, g_pre_mix, g_post_mix, g_pre_ffn, g_post_ffn, w_in, rwkv_conv, w0_fwd, w2_fwd, a0_fwd, a2_fwd, w0_bwd, w2_bwd, a0_bwd, a2_bwd, g2, k_k, k_a, r_k, ln_x_w, ln_x_b, lam_q1, lam_k1, lam_q2, lam_k2, subln_w, w_out, w_up, ffn_conv, ffn_conv_b, w_down):
    depth = w_in.shape[0]
    assert depth == 1, "single-layer block only"
    b, t_lat, d = x.shape
    t_ctx = ctx.shape[1]
    assert t_ctx == TOK_TILE and t_lat % FFN_TILE == 0 and t_lat % GRID_W == 0
    rw = k_k.shape[-1]
    heads = r_k.shape[1]
    d_head = lam_q1.shape[-1]
    width = d - rw
    n_lora = (w2_fwd.shape[1], a2_fwd.shape[1], g2.shape[1])
    assert max(n_lora) <= LANE
    lam_init = 0.8 - 0.6 * math.exp(-0.3 * 0)

    cc = _pad_rows(jnp.concatenate([c, c_ctx[None, :]], axis=0), 2 * SUBLANE)
    mod = _modulation(cc, w_mod[0], b_mod)
    sh1, sc1, gt1, sh2, sc2, gt2 = jnp.split(mod[:b], 6, axis=-1)
    sh1c, sc1c = mod[b, 0:d], mod[b, d:2 * d]
    mod1 = jnp.stack([jnp.stack([jnp.broadcast_to(sh1c, (b, d)), jnp.broadcast_to(sc1c, (b, d))], axis=1),
                      jnp.stack([sh1, sc1], axis=1)], axis=1)
    mod2 = jnp.stack([sh2, sc2], axis=1)

    w = w_in[0]
    o = 3 * rw
    cuts = (o, o + n_lora[0], o + n_lora[0] + n_lora[1], o + sum(n_lora))
    n_rw = o + 3 * LANE
    w_p = jnp.concatenate([w[:, :o], _pad_cols(w[:, cuts[0]:cuts[1]], LANE), _pad_cols(w[:, cuts[1]:cuts[2]], LANE),
                           _pad_cols(w[:, cuts[2]:cuts[3]], LANE), w[:, cuts[3]:]], axis=1).astype(BF16)
    cv = rwkv_conv[0]
    cw = jnp.concatenate([cv[:, :o], _pad_cols(cv[:, cuts[0]:cuts[1]], LANE), _pad_cols(cv[:, cuts[1]:cuts[2]], LANE),
                          _pad_cols(cv[:, cuts[2]:cuts[3]], LANE)], axis=1)
    rope_c, rope_s1, rope_s2 = _rope_tables(t_ctx, t_lat, d_head)

    rwp, q, k, v = _inproj(x, ctx, mod1, g_pre_mix, w_p, rope_c, rope_s1, rope_s2, n_rw, width,
                           float(d_head) ** -0.5 * math.log2(math.e))

    lam_p = jnp.concatenate([lam_q1, lam_k1, lam_q2, lam_k2], axis=0)
    yd = _attention(q, k, v, lam_p, subln_w, t_lat, lam_init)

    w0 = jnp.concatenate([w0_fwd, w0_bwd], axis=0)[:, None, :]
    a0 = jnp.concatenate([a0_fwd, a0_bwd], axis=0)[:, None, :]
    w2p = jnp.stack([_pad_rows(w2_fwd[0], LANE), _pad_rows(w2_bwd[0], LANE)], axis=0).astype(BF16)
    a2p = jnp.stack([_pad_rows(a2_fwd[0], LANE), _pad_rows(a2_bwd[0], LANE)], axis=0).astype(BF16)
    g2p = _pad_rows(g2[0], LANE).astype(BF16)
    rp, bon, gate = _rwkv_build(rwp, cw, k_k, k_a, r_k.reshape(1, rw), w0, w2p, a0, a2p, g2p, t_lat, rw, heads)
    yf, yb = _rwkv_chain(rp, t_lat, rw, heads)

    x1, h2 = _mix_out(yf, yb, bon, gate, ln_x_w, ln_x_b, yd, w_out[0].astype(BF16), g_post_mix, x,
                      gt1[:, None, :], g_pre_ffn, mod2, rw, heads)

    return _ffn(h2, w_up[0].astype(BF16), ffn_conv[0], ffn_conv_b, w_down[0].astype(BF16), x1,
                gt2[:, None, :], g_post_ffn)
```

```python
import functools
import math

import jax
import jax.numpy as jnp
from jax import lax
from jax.experimental import pallas as pl
from jax.experimental.pallas import tpu as pltpu

F32 = jnp.float32
BF16 = jnp.bfloat16

LANE = 128
SUBLANE = 8
MXU_WIDTH = 256
TOK_TILE = 256
CHUNK = 64
BUILD_GROUP_CHUNKS = 2
ATTN_TILE = 512
ATTN_SUB = 256
FFN_TILE = 1024
FFN_HALO = 16
FFN_COLS = 256
FFN_DOWN_GROUP = 11
VMEM_LIMIT = 56 * 1024 * 1024

GRID_W = 64
ROPE_THETA = 10000.0
NORM_EPS = 1e-6
LN_X_EPS = 64e-5
SUBLN_EPS = 1e-5

_NT = (((1,), (1,)), ((), ()))
_TN = (((0,), (0,)), ((), ()))


def _dot(a, b):
    return jnp.dot(a.astype(BF16), b.astype(BF16), preferred_element_type=F32)


def _dot_nt(a, b):
    return lax.dot_general(a.astype(BF16), b.astype(BF16), _NT, preferred_element_type=F32)


def _dot_tn(a, b):
    return lax.dot_general(a.astype(BF16), b.astype(BF16), _TN, preferred_element_type=F32)


def _split(a):
    hi = a.astype(BF16)
    lo = (a - hi.astype(F32)).astype(BF16)
    return hi, lo


def _dot_hl(a, b_exact):
    hi, lo = _split(a)
    return (jnp.dot(hi, b_exact, preferred_element_type=F32)
            + jnp.dot(lo, b_exact, preferred_element_type=F32))


def _dot_lh(a_exact, b):
    hi, lo = _split(b)
    return (jnp.dot(a_exact, hi, preferred_element_type=F32)
            + jnp.dot(a_exact, lo, preferred_element_type=F32))


def _dot3(a, b):
    ah, al = _split(a)
    bh, bl = _split(b)
    return (jnp.dot(ah, bh, preferred_element_type=F32)
            + jnp.dot(ah, bl, preferred_element_type=F32)
            + jnp.dot(al, bh, preferred_element_type=F32))


def _sigmoid(z):
    return 1.0 / (1.0 + jnp.exp(-z))


def _rms(x, g, eps):
    return x * lax.rsqrt(jnp.mean(x * x, axis=-1, keepdims=True) + eps) * g


def _group_ones(width, group):
    i = lax.broadcasted_iota(jnp.int32, (width, width), 0) // group
    j = lax.broadcasted_iota(jnp.int32, (width, width), 1) // group
    return jnp.where(i == j, 1.0, 0.0).astype(BF16)


def _group_sum(x, group, exact=False):
    ones = _group_ones(MXU_WIDTH, group)
    pieces = []
    for c0 in range(0, x.shape[1], MXU_WIDTH):
        piece = x[:, c0:c0 + MXU_WIDTH]
        pieces.append(_dot_hl(piece, ones) if exact else _dot(piece, ones))
    return jnp.concatenate(pieces, axis=1)


def _shift_rows(x, prev_row, next_row):
    n = x.shape[0]
    row = lax.broadcasted_iota(jnp.int32, (SUBLANE, 1), 0)
    down = pltpu.roll(x, 1, axis=0)
    up = pltpu.roll(x, n - 1, axis=0)
    xm1 = jnp.concatenate([jnp.where(row == 0, prev_row, down[:SUBLANE]), down[SUBLANE:]], axis=0)
    xp1 = jnp.concatenate([up[:n - SUBLANE], jnp.where(row == SUBLANE - 1, next_row, up[n - SUBLANE:])], axis=0)
    return xm1, xp1


def _params(*sem):
    return pltpu.CompilerParams(dimension_semantics=sem, vmem_limit_bytes=VMEM_LIMIT)


def _const_spec(shape):
    n = len(shape)
    return pl.BlockSpec(shape, lambda *_: (0,) * n, pipeline_mode=pl.Buffered(1))


def _mod_kernel(c_ref, w_ref, b_ref, o_ref):
    c = c_ref[...]
    o_ref[...] = _dot3(c * _sigmoid(c), w_ref[...]) + b_ref[...]


def _modulation(cc, w_mod, b_mod):
    rows, d = cc.shape
    n = w_mod.shape[1]
    tn = 1024
    return pl.pallas_call(
        _mod_kernel,
        grid=(n // tn,),
        in_specs=[pl.BlockSpec((rows, d), lambda j: (0, 0)),
                  pl.BlockSpec((d, tn), lambda j: (0, j)),
                  pl.BlockSpec((1, tn), lambda j: (0, j))],
        out_specs=pl.BlockSpec((rows, tn), lambda j: (0, j)),
        out_shape=jax.ShapeDtypeStruct((rows, n), F32),
        compiler_params=_params("parallel"),
        name="adaln_mod",
    )(cc, w_mod, b_mod)


def _inproj_kernel(x_ref, ctx_ref, mod_ref, g_ref, w_ref, rc_ref, rs1_ref, rs2_ref,
                   rw_ref, q_ref, k_ref, v_ref, *, n_rw, width, scale):
    t = pl.program_id(1)
    xin = jnp.where(t == 0, ctx_ref[0], x_ref[0])
    mod = mod_ref[0, 0]
    h = (_rms(xin, g_ref[...], NORM_EPS) * (1.0 + mod[1:2]) + mod[0:1]).astype(BF16)

    step = 4 * LANE
    for c0 in range(0, n_rw, step):
        c1 = min(c0 + step, n_rw)
        rw_ref[0, :, c0:c1] = jnp.dot(h, w_ref[:, c0:c1], preferred_element_type=F32)

    reps = width // LANE
    cos = jnp.concatenate([rc_ref[...]] * reps, axis=1)
    s1 = jnp.concatenate([rs1_ref[...]] * reps, axis=1)
    s2 = jnp.concatenate([rs2_ref[...]] * reps, axis=1)
    half = GRID_W // 2

    def rope(z):
        return z * cos + pltpu.roll(z, width - half, axis=1) * s1 + pltpu.roll(z, half, axis=1) * s2

    q = jnp.dot(h, w_ref[:, n_rw:n_rw + width], preferred_element_type=F32)
    q_ref[0] = (rope(q) * scale).astype(BF16)
    k = jnp.dot(h, w_ref[:, n_rw + width:n_rw + 2 * width], preferred_element_type=F32)
    k_ref[0] = rope(k).astype(BF16)
    v = jnp.dot(h, w_ref[:, n_rw + 2 * width:n_rw + 3 * width], preferred_element_type=F32)
    v_ref[0] = v.astype(BF16)


def _inproj(x, ctx, mod1, g_pre, w_p, rope_c, rope_s1, rope_s2, n_rw, width, scale):
    b, t_lat, d = x.shape
    n_ctx = ctx.shape[1] // TOK_TILE
    n_tiles = t_lat // TOK_TILE + n_ctx
    t_all = n_tiles * TOK_TILE
    tok = lambda w: pl.BlockSpec((1, TOK_TILE, w), lambda i, t: (i, t, 0))
    return pl.pallas_call(
        functools.partial(_inproj_kernel, n_rw=n_rw, width=width, scale=scale),
        grid=(b, n_tiles),
        in_specs=[pl.BlockSpec((1, TOK_TILE, d), lambda i, t: (i, jnp.maximum(t - 1, 0), 0)),
                  pl.BlockSpec((1, TOK_TILE, d), lambda i, t: (i, 0, 0)),
                  pl.BlockSpec((1, 1, 2, d), lambda i, t: (i, jnp.minimum(t, 1), 0, 0)),
                  _const_spec((1, d)),
                  _const_spec(w_p.shape),
                  pl.BlockSpec((TOK_TILE, LANE), lambda i, t: (t, 0)),
                  pl.BlockSpec((TOK_TILE, LANE), lambda i, t: (t, 0)),
                  pl.BlockSpec((TOK_TILE, LANE), lambda i, t: (t, 0))],
        out_specs=[tok(n_rw),
                   pl.BlockSpec((1, TOK_TILE, width), lambda i, t: (i, jnp.maximum(t - n_ctx, 0), 0)),
                   tok(width), tok(width)],
        out_shape=[jax.ShapeDtypeStruct((b, t_all, n_rw), F32),
                   jax.ShapeDtypeStruct((b, t_lat, width), BF16),
                   jax.ShapeDtypeStruct((b, t_all, width), BF16),
                   jax.ShapeDtypeStruct((b, t_all, width), BF16)],
        compiler_params=_params("parallel", "arbitrary"),
        name="in_proj",
    )(x, ctx, mod1, g_pre, w_p, rope_c, rope_s1, rope_s2)


def _attn_kernel(q_ref, k_ref, v_ref, lam_ref, g_ref, o_ref, *, lam_init):
    lp = lam_ref[...]
    lam = (jnp.exp(jnp.sum(lp[0:1] * lp[1:2], axis=-1, keepdims=True))
           - jnp.exp(jnp.sum(lp[2:3] * lp[3:4], axis=-1, keepdims=True)) + lam_init)
    k = k_ref[0]
    d = k.shape[-1] // 2
    lane = lax.broadcasted_iota(jnp.int32, (ATTN_SUB, 2 * d), 1)
    zero = jnp.zeros((ATTN_SUB, 2 * d), BF16)

    scores = []
    for i in range(ATTN_TILE // ATTN_SUB):
        q = q_ref[0, i * ATTN_SUB:(i + 1) * ATTN_SUB, :]
        scores.append([lax.dot_general(jnp.where(sel, q, zero), k, _NT, preferred_element_type=F32)
                       for sel in (lane < d, lane >= d)])
    v_ones = jnp.concatenate([v_ref[0], jnp.ones_like(v_ref[0])], axis=1)
    for i, (s1, s2) in enumerate(scores):
        outs = []
        for s in (s1, s2):
            p = jnp.exp2(s - jnp.max(s, axis=-1, keepdims=True)).astype(BF16)
            ol = jnp.dot(p, v_ones, preferred_element_type=F32)
            outs.append(ol[:, :2 * d] / ol[:, 2 * d:2 * d + 1])
        o = outs[0] - lam * outs[1]
        o_ref[0, i * ATTN_SUB:(i + 1) * ATTN_SUB, :] = (
            _rms(o, g_ref[...], SUBLN_EPS) * (1.0 - lam_init)).astype(BF16)


def _attention(q, k, v, lam_p, subln_w, t_lat, lam_init):
    b, t_all, width = k.shape
    hd = subln_w.shape[-1]
    heads = width // hd
    return pl.pallas_call(
        functools.partial(_attn_kernel, lam_init=lam_init),
        grid=(b, heads, t_lat // ATTN_TILE),
        in_specs=[pl.BlockSpec((1, ATTN_TILE, hd), lambda i, h, j: (i, j, h)),
                  pl.BlockSpec((1, t_all, hd), lambda i, h, j: (i, 0, h)),
                  pl.BlockSpec((1, t_all, hd), lambda i, h, j: (i, 0, h)),
                  _const_spec(lam_p.shape),
                  _const_spec((1, hd))],
        out_specs=pl.BlockSpec((1, ATTN_TILE, hd), lambda i, h, j: (i, j, h)),
        out_shape=jax.ShapeDtypeStruct((b, t_lat, width), BF16),
        compiler_params=_params("parallel", "parallel", "arbitrary"),
        name="diff_attn",
    )(q, k, v, lam_p, subln_w)


def _build_kernel(main_ref, prev_ref, next_ref, cw_ref, kk_ref, ka_ref, rk_ref,
                  w0_ref, w2_ref, a0_ref, a2_ref, g2_ref,
                  rp_ref, bon_ref, gate_ref,
                  khz_s, rhz_s, vz_s, gamz_s, kt_s, bt_s, kb_s, bb_s, *, n_tiles, rw, heads):
    tile = pl.program_id(1)
    hd = rw // heads
    n_chunks = TOK_TILE // CHUNK

    group_chunks = BUILD_GROUP_CHUNKS
    group_rows = group_chunks * CHUNK
    n_groups = n_chunks // group_chunks
    starts = (tile == 0) | (tile == 1)
    ends = (tile == 0) | (tile == n_tiles - 1)
    cw = cw_ref[...]
    ri = lax.broadcasted_iota(jnp.int32, (group_rows, group_rows), 0)
    ci = lax.broadcasted_iota(jnp.int32, (group_rows, group_rows), 1)
    same = (ri // CHUNK) == (ci // CHUNK)

    def prepare(g):
        r0 = g * group_rows
        rows = slice(r0, r0 + group_rows)
        x = main_ref[0, rows, :]
        if g > 0:
            prev_row = main_ref[0, r0 - 1:r0, :]
        else:
            prev_row = jnp.where(starts, 0.0, prev_ref[0][SUBLANE - 1:SUBLANE, :])
        if g < n_groups - 1:
            next_row = main_ref[0, r0 + group_rows:r0 + group_rows + 1, :]
        else:
            next_row = jnp.where(ends, 0.0, next_ref[0][0:1, :])
        xm1, xp1 = _shift_rows(x, prev_row, next_row)
        xc = cw[0:1] * xm1 + cw[1:2] * x + cw[2:3] * xp1

        r = xc[:, 0:rw]
        k = xc[:, rw:2 * rw]
        v = xc[:, 2 * rw:3 * rw]
        wl = xc[:, 3 * rw:3 * rw + LANE]
        al = xc[:, 3 * rw + LANE:3 * rw + 2 * LANE]
        gl = xc[:, 3 * rw + 2 * LANE:3 * rw + 3 * LANE]
        gate_ref[0, rows, :] = _dot(_sigmoid(gl), g2_ref[...])

        kk = k * kk_ref[...]
        kk = kk / jnp.maximum(jnp.sqrt(_group_sum(kk * kk, hd)), 1e-12)
        w_lat = jnp.tanh(wl).astype(BF16)
        al_b = al.astype(BF16)
        bonus = jnp.zeros((group_rows, rw), F32)
        for d in range(2):
            z = w0_ref[d] + jnp.dot(w_lat, w2_ref[d], preferred_element_type=F32)
            ld = -math.exp(-0.5) * _sigmoid(z)
            a = _sigmoid(a0_ref[d] + jnp.dot(al_b, a2_ref[d], preferred_element_type=F32))
            kd = k * (1.0 + (a - 1.0) * ka_ref[...])
            bvec = kk * a
            bonus = bonus + 0.5 * _group_sum(r * kd * rk_ref[...], hd) * v

            before = (ci <= ri) if d == 0 else (ci >= ri)
            cum = _dot_lh(jnp.where(same & before, 1.0, 0.0).astype(BF16), ld)
            last = [c * CHUNK + CHUNK - 1 if d == 0 else c * CHUNK for c in range(group_chunks)]
            ctot = jnp.concatenate([jnp.broadcast_to(cum[i:i + 1], (CHUNK, rw)) for i in last], axis=0)
            e_pos = jnp.exp(cum)
            e_neg = jnp.exp(-cum)
            e_rem = jnp.exp(ctot - cum)
            gam = jnp.exp(jnp.concatenate([ctot[c * CHUNK:c * CHUNK + SUBLANE] for c in range(group_chunks)],
                                          axis=0))
            narrow = ((kt_s, kd * e_neg), (bt_s, bvec * e_neg), (kb_s, kd * e_rem), (bb_s, bvec * e_rem))
            wide_lo = ((khz_s, kk * jnp.exp(cum - ld)), (rhz_s, r * e_pos))
            g_rows = slice(g * group_chunks * SUBLANE, (g + 1) * group_chunks * SUBLANE)
            for h in range(heads):
                sl = slice(h * hd, (h + 1) * hd)
                for ref, val in narrow:
                    ref[d * heads + h, rows, :] = val[:, sl].astype(BF16)
                for ref, val in wide_lo:
                    lo = val[:, sl]
                    ref[d * heads + h, rows, :] = jnp.concatenate([lo, jnp.zeros_like(lo)], axis=1).astype(BF16)
                lo = gam[:, sl]
                gamz_s[d * heads + h, g_rows, :] = jnp.concatenate([lo, jnp.zeros_like(lo)], axis=1)
        bon_ref[0, rows, :] = bonus
        for h in range(heads):
            hi = v[:, h * hd:(h + 1) * hd]
            vz_s[h, rows, :] = jnp.concatenate([jnp.zeros_like(hi), hi], axis=1).astype(BF16)

    ii = lax.broadcasted_iota(jnp.int32, (CHUNK, CHUNK), 0)
    jj = lax.broadcasted_iota(jnp.int32, (CHUNK, CHUNK), 1)
    eye_f = jnp.where(ii == jj, 1.0, 0.0)
    level_masks = []
    size = 1
    while size < CHUNK:
        level_masks.append(((ii // (2 * size)) == (jj // (2 * size))) & ((ii // size) != (jj // size)))
        size *= 2
    iw = lax.broadcasted_iota(jnp.int32, (CHUNK, 2 * CHUNK), 0)
    jw = lax.broadcasted_iota(jnp.int32, (CHUNK, 2 * CHUNK), 1)
    jw = jnp.where(jw >= CHUNK, jw - CHUNK, jw)
    strict = (jw < iw, jw > iw)
    incl = (jw <= iw, jw >= iw)
    eye_wide = (lax.broadcasted_iota(jnp.int32, (hd, 2 * hd), 0)
                == lax.broadcasted_iota(jnp.int32, (hd, 2 * hd), 1))
    zero_top = jnp.zeros((CHUNK, 2 * hd), BF16)

    def build_group(chains):
        n = range(len(chains))
        rows = [slice(c * CHUNK, (c + 1) * CHUNK) for _, _, c in chains]
        ld_ = lambda ref: [ref[d * heads + h, rows[i], :] for i, (d, h, _) in enumerate(chains)]
        khz, rhz, kt, bt, kb, bb = (ld_(ref) for ref in (khz_s, rhz_s, kt_s, bt_s, kb_s, bb_s))
        vz = [vz_s[h, rows[i], :] for i, (_, h, _) in enumerate(chains)]
        g_row = [gamz_s[d * heads + h, c * SUBLANE:c * SUBLANE + 1, :] for d, h, c in chains]
        gram = [_dot_nt(jnp.concatenate([khz[i][:, :hd], rhz[i][:, :hd]], axis=0),
                        jnp.concatenate([bt[i], kt[i]], axis=0)) for i in n]
        mm = [jnp.where(strict[chains[i][0]], gram[i][:CHUNK], 0.0) for i in n]
        aa = [jnp.where(incl[chains[i][0]], gram[i][CHUNK:], 0.0) for i in n]
        m_ab = [mm[i][:, :CHUNK] for i in n]
        tinv = [eye_f - jnp.where(level_masks[0], m_ab[i], 0.0) for i in n]
        for lm in level_masks[1:]:
            right = [_dot(jnp.where(lm, m_ab[i], 0.0), tinv[i]) for i in n]
            tinv = [tinv[i] - _dot(tinv[i], right[i]) for i in n]
        zmat = [khz[i].astype(F32) + _dot(mm[i], jnp.concatenate([zero_top, vz[i]], axis=0)) for i in n]
        xmat = [_dot(tinv[i], zmat[i]) for i in n]
        rhs = [jnp.concatenate([(-xmat[i]).astype(BF16), vz[i]], axis=0) for i in n]
        for i, (d, h, c) in enumerate(chains):
            idx = (d * heads + h) * n_chunks + c
            ry = rhz[i].astype(F32) + _dot(aa[i], rhs[i])
            pq = jnp.where(eye_wide, g_row[i], 0.0) + _dot_tn(jnp.concatenate([bb[i], kb[i]], axis=0), rhs[i])
            rp_ref[0, 0, idx, 0:CHUNK, :] = ry.astype(BF16)
            rp_ref[0, 0, idx, CHUNK:CHUNK + hd, :] = pq.astype(BF16)

    def group_chains(g):
        return [(d, h, c) for d in range(2) for c in range(g * group_chunks, (g + 1) * group_chunks)
                for h in range(heads)]

    for g in range(n_groups):
        prepare(g)
        build_group(group_chains(g))


def _rwkv_build(rwp, cw, k_k, k_a, r_k, w0, w2p, a0, a2p, g2p, t_lat, rw, heads):
    b, t_all, n_rw = rwp.shape
    n_tiles = t_all // TOK_TILE
    n_ctx = n_tiles - t_lat // TOK_TILE
    hd = rw // heads
    halo = TOK_TILE // SUBLANE
    n_halo = t_all // SUBLANE
    n_chunks = TOK_TILE // CHUNK
    n_mat = 2 * heads * n_chunks
    narrow = pltpu.VMEM((2 * heads, TOK_TILE, hd), BF16)
    wide = pltpu.VMEM((2 * heads, TOK_TILE, 2 * hd), BF16)
    lat = pl.BlockSpec((1, TOK_TILE, rw), lambda i, t: (i, jnp.maximum(t - n_ctx, 0), 0))
    return pl.pallas_call(
        functools.partial(_build_kernel, n_tiles=n_tiles, rw=rw, heads=heads),
        grid=(b, n_tiles),
        in_specs=[pl.BlockSpec((1, TOK_TILE, n_rw), lambda i, t: (i, t, 0)),
                  pl.BlockSpec((1, SUBLANE, n_rw), lambda i, t: (i, jnp.maximum(t * halo - 1, 0), 0)),
                  pl.BlockSpec((1, SUBLANE, n_rw), lambda i, t: (i, jnp.minimum((t + 1) * halo, n_halo - 1), 0)),
                  _const_spec(cw.shape), _const_spec((1, rw)), _const_spec((1, rw)), _const_spec((1, rw)),
                  _const_spec(w0.shape), _const_spec(w2p.shape), _const_spec(a0.shape), _const_spec(a2p.shape),
                  _const_spec(g2p.shape)],
        out_specs=[pl.BlockSpec((1, 1, n_mat, CHUNK + hd, 2 * hd), lambda i, t: (i, t, 0, 0, 0)), lat, lat],
        out_shape=[jax.ShapeDtypeStruct((b, n_tiles, n_mat, CHUNK + hd, 2 * hd), BF16),
                   jax.ShapeDtypeStruct((b, t_lat, rw), F32),
                   jax.ShapeDtypeStruct((b, t_lat, rw), F32)],
        scratch_shapes=[wide, wide, pltpu.VMEM((heads, TOK_TILE, 2 * hd), BF16),
                        pltpu.VMEM((2 * heads, n_chunks * SUBLANE, 2 * hd), F32),
                        narrow, narrow, narrow, narrow],
        compiler_params=_params("parallel", "arbitrary"),
        name="rwkv_build",
    )(rwp, rwp, rwp, cw, k_k, k_a, r_k, w0, w2p, a0, a2p, g2p)


def _chain_kernel(rpf_ref, rpb_ref, yf_ref, yb_ref, h_s, *, heads, hd):
    n_chunks = TOK_TILE // CHUNK

    @pl.when(pl.program_id(1) == 0)
    def _():
        h_s[...] = jnp.zeros_like(h_s)

    ii = lax.broadcasted_iota(jnp.int32, (hd, hd), 0)
    jj = lax.broadcasted_iota(jnp.int32, (hd, hd), 1)
    eye_b = jnp.where(ii == jj, 1.0, 0.0).astype(BF16)
    state = [h_s[i] for i in range(2 * heads)]
    for j in range(n_chunks):
        jobs = [(0, rpf_ref, yf_ref, j), (1, rpb_ref, yb_ref, n_chunks - 1 - j)]
        out = [[jnp.dot(ref[0, 0, h * n_chunks + c],
                        jnp.concatenate([state[d * heads + h].astype(BF16), eye_b], axis=0),
                        preferred_element_type=F32) for h in range(heads)] for d, ref, _, c in jobs]
        for d, _, y_ref, c in jobs:
            for h in range(heads):
                state[d * heads + h] = out[d][h][CHUNK:]
                y_ref[0, c * CHUNK:(c + 1) * CHUNK, h * hd:(h + 1) * hd] = out[d][h][:CHUNK]
    for i in range(2 * heads):
        h_s[i] = state[i]


def _rwkv_chain(rp, t_lat, rw, heads):
    b, n_tiles, n_mat = rp.shape[:3]
    n_lat = t_lat // TOK_TILE
    hd = rw // heads
    per_dir = n_mat // 2

    def bwd_tile(s):
        return jnp.where(s == 0, 0, n_tiles - s)

    mats = lambda d, tile: pl.BlockSpec((1, 1, per_dir, CHUNK + hd, 2 * hd),
                                        lambda i, s: (i, tile(s), d, 0, 0))
    yf_spec = pl.BlockSpec((1, TOK_TILE, rw), lambda i, s: (i, jnp.maximum(s - (n_tiles - n_lat), 0), 0))
    yb_spec = pl.BlockSpec((1, TOK_TILE, rw),
                           lambda i, s: (i, jnp.where(s == 0, n_lat - 1, bwd_tile(s) - (n_tiles - n_lat)), 0))
    return pl.pallas_call(
        functools.partial(_chain_kernel, heads=heads, hd=hd),
        grid=(b, n_tiles),
        in_specs=[mats(0, lambda s: s), mats(1, bwd_tile)],
        out_specs=[yf_spec, yb_spec],
        out_shape=[jax.ShapeDtypeStruct((b, t_lat, rw), F32), jax.ShapeDtypeStruct((b, t_lat, rw), F32)],
        scratch_shapes=[pltpu.VMEM((2 * heads, hd, hd), F32)],
        compiler_params=_params("parallel", "arbitrary"),
        name="rwkv_chain",
    )(rp, rp)


def _mix_kernel(yf_ref, yb_ref, bon_ref, gate_ref, lnw_ref, lnb_ref, yd_ref, wo_ref, gpost_ref,
                x_ref, gt_ref, gpre_ref, mod_ref, x1_ref, h2_ref, *, rw, heads):
    hd = rw // heads
    y = yf_ref[0] + yb_ref[0]
    mu = _group_sum(y, hd, exact=True) * (1.0 / hd)
    yc = y - mu
    var = _group_sum(yc * yc, hd, exact=True) * (1.0 / hd)
    yn = yc * lax.rsqrt(var + LN_X_EPS) * lnw_ref[...] + lnb_ref[...]
    yr = ((yn + bon_ref[0]) * gate_ref[0]).astype(BF16)
    m = (jnp.dot(yr, wo_ref[0:rw, :], preferred_element_type=F32)
         + jnp.dot(yd_ref[0], wo_ref[rw:, :], preferred_element_type=F32))
    x1 = x_ref[0] + gt_ref[0] * _rms(m, gpost_ref[...], NORM_EPS)
    x1_ref[0] = x1
    mod = mod_ref[0]
    h2_ref[0] = (_rms(x1, gpre_ref[...], NORM_EPS) * (1.0 + mod[1:2]) + mod[0:1]).astype(BF16)


def _mix_out(yf, yb, bon, gate, ln_w, ln_b, yd, w_out, g_post, x, gt1, g_pre2, mod2, rw, heads):
    b, t_lat, d = x.shape
    tok = lambda w: pl.BlockSpec((1, TOK_TILE, w), lambda i, j: (i, j, 0))
    return pl.pallas_call(
        functools.partial(_mix_kernel, rw=rw, heads=heads),
        grid=(b, t_lat // TOK_TILE),
        in_specs=[tok(rw), tok(rw), tok(rw), tok(rw),
                  _const_spec((1, rw)), _const_spec((1, rw)),
                  tok(d - rw), _const_spec((d, d)), _const_spec((1, d)),
                  tok(d),
                  pl.BlockSpec((1, 1, d), lambda i, j: (i, 0, 0)),
                  _const_spec((1, d)),
                  pl.BlockSpec((1, 2, d), lambda i, j: (i, 0, 0))],
        out_specs=[tok(d), tok(d)],
        out_shape=[jax.ShapeDtypeStruct((b, t_lat, d), F32),
                   jax.ShapeDtypeStruct((b, t_lat, d), BF16)],
        compiler_params=_params("parallel", "arbitrary"),
        name="mix_out",
    )(yf, yb, bon, gate, ln_w, ln_b, yd, w_out, g_post, x, gt1, g_pre2, mod2)


def _ffn_kernel(h_ref, hp_ref, hn_ref, wu_ref, cw_ref, cb_ref, wd_ref, x1_ref, gt_ref, gpost_ref,
                o_ref, act_s, *, n_tiles, d_ff):
    j = pl.program_id(1)
    first = j == 0
    last = j == n_tiles - 1
    lhs = jnp.concatenate([h_ref[0], hp_ref[0], hn_ref[0]], axis=0)
    prev_at = FFN_TILE + FFN_HALO - 1
    next_at = FFN_TILE + FFN_HALO
    starts = list(range(0, d_ff, FFN_COLS))

    def up(c0):
        return [jnp.dot(lhs, wu_ref[:, base:base + FFN_COLS], preferred_element_type=F32)
                for base in (c0, d_ff + c0)]

    def conv(u, base):
        cols = slice(base, base + FFN_COLS)
        main = u[:FFN_TILE]
        um1, up1 = _shift_rows(main, jnp.where(first, 0.0, u[prev_at:prev_at + 1]),
                               jnp.where(last, 0.0, u[next_at:next_at + 1]))
        cw = cw_ref[:, cols]
        return cw[0:1] * um1 + cw[1:2] * main + cw[2:3] * up1 + cb_ref[:, cols]

    acc = None
    raw = up(starts[0])
    for i, c0 in enumerate(starts):
        nxt = up(starts[i + 1]) if i + 1 < len(starts) else None
        val = conv(raw[0], c0)
        gate = conv(raw[1], d_ff + c0)
        act_s[:, c0:c0 + FFN_COLS] = (val * gate * _sigmoid(gate)).astype(BF16)
        raw = nxt
        if (i + 1) % FFN_DOWN_GROUP == 0 or i + 1 == len(starts):
            k0 = (i // FFN_DOWN_GROUP) * FFN_DOWN_GROUP * FFN_COLS
            part = jnp.dot(act_s[:, k0:c0 + FFN_COLS], wd_ref[k0:c0 + FFN_COLS, :], preferred_element_type=F32)
            acc = part if acc is None else acc + part
    o_ref[0] = x1_ref[0] + gt_ref[0] * _rms(acc, gpost_ref[...], NORM_EPS)


def _ffn(h2, w_up, conv_w, conv_b, w_down, x1, gt2, g_post):
    b, t_lat, d = x1.shape
    d_ff = w_down.shape[0]
    n_tiles = t_lat // FFN_TILE
    halo = FFN_TILE // FFN_HALO
    n_halo = t_lat // FFN_HALO
    tok = pl.BlockSpec((1, FFN_TILE, d), lambda i, j: (i, j, 0))
    return pl.pallas_call(
        functools.partial(_ffn_kernel, n_tiles=n_tiles, d_ff=d_ff),
        grid=(b, n_tiles),
        in_specs=[tok,
                  pl.BlockSpec((1, FFN_HALO, d), lambda i, j: (i, jnp.maximum(j * halo - 1, 0), 0)),
                  pl.BlockSpec((1, FFN_HALO, d), lambda i, j: (i, jnp.minimum((j + 1) * halo, n_halo - 1), 0)),
                  _const_spec(w_up.shape), _const_spec(conv_w.shape), _const_spec(conv_b.shape),
                  _const_spec(w_down.shape),
                  tok,
                  pl.BlockSpec((1, 1, d), lambda i, j: (i, 0, 0)),
                  _const_spec((1, d))],
        out_specs=tok,
        out_shape=jax.ShapeDtypeStruct((b, t_lat, d), F32),
        scratch_shapes=[pltpu.VMEM((FFN_TILE, d_ff), BF16)],
        compiler_params=_params("parallel", "arbitrary"),
        name="conv_ffn",
    )(h2, h2, h2, w_up, conv_w, conv_b, w_down, x1, gt2, g_post)


def _rope_tables(t_ctx, t_lat, d_head):
    n_pair = d_head // 4
    pos = jnp.arange(t_lat, dtype=F32)
    row = jnp.floor(pos / GRID_W)
    col = pos - row * GRID_W
    inv = ROPE_THETA ** (-jnp.arange(n_pair, dtype=F32) / n_pair)
    ang = jnp.concatenate([row[:, None] * inv, col[:, None] * inv], axis=-1)
    ang = jnp.concatenate([jnp.zeros((t_ctx, d_head // 2), F32), ang], axis=0)
    cos, sin = jnp.cos(ang), jnp.sin(ang)
    zero = jnp.zeros_like(sin)
    reps = LANE // d_head
    tile = lambda a, b_: jnp.tile(jnp.concatenate([a, b_], axis=-1), (1, reps))
    return tile(cos, cos), tile(-sin, zero), tile(zero, sin)


def _pad_cols(a, width):
    return jnp.pad(a, ((0, 0), (0, width - a.shape[1])))


def _pad_rows(a, height):
    return jnp.pad(a, ((0, height - a.shape[0]), (0, 0)))


def kernel(x, c, ctx, c_ctx, w_mod, b_mod, g_pre_mix, g_post_mix, g_pre_ffn, g_post_ffn, w_in, rwkv_conv, w0_fwd, w2_fwd, a0_fwd, a2_fwd, w0_bwd, w2_bwd, a0_bwd, a2_bwd, g2, k_k, k_a, r_k, ln_x_w, ln_x_b, lam_q1, lam_k1, lam_q2, lam_k2, subln_w, w_out, w_up, ffn_conv, ffn_conv_b, w_down):
    depth = w_in.shape[0]
    assert depth == 1, "single-layer block only"
    b, t_lat, d = x.shape
    t_ctx = ctx.shape[1]
    assert t_ctx == TOK_TILE and t_lat % FFN_TILE == 0 and t_lat % GRID_W == 0
    rw = k_k.shape[-1]
    heads = r_k.shape[1]
    d_head = lam_q1.shape[-1]
    width = d - rw
    n_lora = (w2_fwd.shape[1], a2_fwd.shape[1], g2.shape[1])
    assert max(n_lora) <= LANE
    lam_init = 0.8 - 0.6 * math.exp(-0.3 * 0)

    cc = _pad_rows(jnp.concatenate([c, c_ctx[None, :]], axis=0), 2 * SUBLANE)
    mod = _modulation(cc, w_mod[0], b_mod)
    sh1, sc1, gt1, sh2, sc2, gt2 = jnp.split(mod[:b], 6, axis=-1)
    sh1c, sc1c = mod[b, 0:d], mod[b, d:2 * d]
    mod1 = jnp.stack([jnp.stack([jnp.broadcast_to(sh1c, (b, d)), jnp.broadcast_to(sc1c, (b, d))], axis=1),
                      jnp.stack([sh1, sc1], axis=1)], axis=1)
    mod2 = jnp.stack([sh2, sc2], axis=1)

    w = w_in[0]
    o = 3 * rw
    cuts = (o, o + n_lora[0], o + n_lora[0] + n_lora[1], o + sum(n_lora))
    n_rw = o + 3 * LANE
    w_p = jnp.concatenate([w[:, :o], _pad_cols(w[:, cuts[0]:cuts[1]], LANE), _pad_cols(w[:, cuts[1]:cuts[2]], LANE),
                           _pad_cols(w[:, cuts[2]:cuts[3]], LANE), w[:, cuts[3]:]], axis=1).astype(BF16)
    cv = rwkv_conv[0]
    cw = jnp.concatenate([cv[:, :o], _pad_cols(cv[:, cuts[0]:cuts[1]], LANE), _pad_cols(cv[:, cuts[1]:cuts[2]], LANE),
                          _pad_cols(cv[:, cuts[2]:cuts[3]], LANE)], axis=1)
    rope_c, rope_s1, rope_s2 = _rope_tables(t_ctx, t_lat, d_head)

    rwp, q, k, v = _inproj(x, ctx, mod1, g_pre_mix, w_p, rope_c, rope_s1, rope_s2, n_rw, width,
                           float(d_head) ** -0.5 * math.log2(math.e))

    lam_p = jnp.concatenate([lam_q1, lam_k1, lam_q2, lam_k2], axis=0)
    yd = _attention(q, k, v, lam_p, subln_w, t_lat, lam_init)

    w0 = jnp.concatenate([w0_fwd, w0_bwd], axis=0)[:, None, :]
    a0 = jnp.concatenate([a0_fwd, a0_bwd], axis=0)[:, None, :]
    w2p = jnp.stack([_pad_rows(w2_fwd[0], LANE), _pad_rows(w2_bwd[0], LANE)], axis=0).astype(BF16)
    a2p = jnp.stack([_pad_rows(a2_fwd[0], LANE), _pad_rows(a2_bwd[0], LANE)], axis=0).astype(BF16)
    g2p = _pad_rows(g2[0], LANE).astype(BF16)
    rp, bon, gate = _rwkv_build(rwp, cw, k_k, k_a, r_k.reshape(1, rw), w0, w2p, a0, a2p, g2p, t_lat, rw, heads)
    yf, yb = _rwkv_chain(rp, t_lat, rw, heads)

    x1, h2 = _mix_out(yf, yb, bon, gate, ln_x_w, ln_x_b, yd, w_out[0].astype(BF16), g_post_mix, x,
                      gt1[:, None, :], g_pre_ffn, mod2, rw, heads)

    return _ffn(h2, w_up[0].astype(BF16), ffn_conv[0], ffn_conv_b, w_down[0].astype(BF16), x1,
                gt2[:, None, :], g_post_ffn)
```

```python
import functools
import math

import jax
import jax.numpy as jnp
from jax import lax
from jax.experimental import pallas as pl
from jax.experimental.pallas import tpu as pltpu

F32 = jnp.float32
BF16 = jnp.bfloat16

LANE = 128
SUBLANE = 8
MXU_WIDTH = 256
TOK_TILE = 256
CHUNK = 64
BUILD_GROUP_CHUNKS = 2
ATTN_TILE = 512
ATTN_SUB = 256
MIX_TILE = 512
FFN_TILE = 1024
FFN_HALO = 16
FFN_COLS = 256
FFN_DOWN_GROUP = 11
VMEM_LIMIT = 56 * 1024 * 1024

GRID_W = 64
ROPE_THETA = 10000.0
NORM_EPS = 1e-6
LN_X_EPS = 64e-5
SUBLN_EPS = 1e-5

_NT = (((1,), (1,)), ((), ()))
_TN = (((0,), (0,)), ((), ()))


def _dot(a, b):
    return jnp.dot(a.astype(BF16), b.astype(BF16), preferred_element_type=F32)


def _dot_nt(a, b):
    return lax.dot_general(a.astype(BF16), b.astype(BF16), _NT, preferred_element_type=F32)


def _dot_tn(a, b):
    return lax.dot_general(a.astype(BF16), b.astype(BF16), _TN, preferred_element_type=F32)


def _split(a):
    hi = a.astype(BF16)
    lo = (a - hi.astype(F32)).astype(BF16)
    return hi, lo


def _dot_hl(a, b_exact):
    hi, lo = _split(a)
    return (jnp.dot(hi, b_exact, preferred_element_type=F32)
            + jnp.dot(lo, b_exact, preferred_element_type=F32))


def _dot_lh(a_exact, b):
    hi, lo = _split(b)
    return (jnp.dot(a_exact, hi, preferred_element_type=F32)
            + jnp.dot(a_exact, lo, preferred_element_type=F32))


def _dot3(a, b):
    ah, al = _split(a)
    bh, bl = _split(b)
    return (jnp.dot(ah, bh, preferred_element_type=F32)
            + jnp.dot(ah, bl, preferred_element_type=F32)
            + jnp.dot(al, bh, preferred_element_type=F32))


def _sigmoid(z):
    return 1.0 / (1.0 + jnp.exp(-z))


def _rms(x, g, eps):
    return x * lax.rsqrt(jnp.mean(x * x, axis=-1, keepdims=True) + eps) * g


def _group_ones(width, group):
    i = lax.broadcasted_iota(jnp.int32, (width, width), 0) // group
    j = lax.broadcasted_iota(jnp.int32, (width, width), 1) // group
    return jnp.where(i == j, 1.0, 0.0).astype(BF16)


def _group_sum(x, group, exact=False):
    ones = _group_ones(MXU_WIDTH, group)
    pieces = []
    for c0 in range(0, x.shape[1], MXU_WIDTH):
        piece = x[:, c0:c0 + MXU_WIDTH]
        pieces.append(_dot_hl(piece, ones) if exact else _dot(piece, ones))
    return jnp.concatenate(pieces, axis=1)


def _shift_rows(x, prev_row, next_row):
    n = x.shape[0]
    row = lax.broadcasted_iota(jnp.int32, (SUBLANE, 1), 0)
    down = pltpu.roll(x, 1, axis=0)
    up = pltpu.roll(x, n - 1, axis=0)
    xm1 = jnp.concatenate([jnp.where(row == 0, prev_row, down[:SUBLANE]), down[SUBLANE:]], axis=0)
    xp1 = jnp.concatenate([up[:n - SUBLANE], jnp.where(row == SUBLANE - 1, next_row, up[n - SUBLANE:])], axis=0)
    return xm1, xp1


def _params(*sem):
    return pltpu.CompilerParams(dimension_semantics=sem, vmem_limit_bytes=VMEM_LIMIT)


def _const_spec(shape):
    n = len(shape)
    return pl.BlockSpec(shape, lambda *_: (0,) * n, pipeline_mode=pl.Buffered(1))


def _mod_kernel(c_ref, w_ref, b_ref, o_ref):
    c = c_ref[...]
    o_ref[...] = _dot3(c * _sigmoid(c), w_ref[...]) + b_ref[...]


def _modulation(cc, w_mod, b_mod):
    rows, d = cc.shape
    n = w_mod.shape[1]
    tn = 1024
    return pl.pallas_call(
        _mod_kernel,
        grid=(n // tn,),
        in_specs=[pl.BlockSpec((rows, d), lambda j: (0, 0)),
                  pl.BlockSpec((d, tn), lambda j: (0, j)),
                  pl.BlockSpec((1, tn), lambda j: (0, j))],
        out_specs=pl.BlockSpec((rows, tn), lambda j: (0, j)),
        out_shape=jax.ShapeDtypeStruct((rows, n), F32),
        compiler_params=_params("parallel"),
        name="adaln_mod",
    )(cc, w_mod, b_mod)


def _inproj_kernel(x_ref, ctx_ref, mod_ref, g_ref, w_ref, rc_ref, rs1_ref, rs2_ref,
                   rw_ref, q_ref, k_ref, v_ref, *, n_rw, width, scale):
    t = pl.program_id(1)
    xin = jnp.where(t == 0, ctx_ref[0], x_ref[0])
    mod = mod_ref[0, 0]
    h = (_rms(xin, g_ref[...], NORM_EPS) * (1.0 + mod[1:2]) + mod[0:1]).astype(BF16)

    step = 4 * LANE
    for c0 in range(0, n_rw, step):
        c1 = min(c0 + step, n_rw)
        rw_ref[0, :, c0:c1] = jnp.dot(h, w_ref[:, c0:c1], preferred_element_type=F32)

    reps = width // LANE
    cos = jnp.concatenate([rc_ref[...]] * reps, axis=1)
    s1 = jnp.concatenate([rs1_ref[...]] * reps, axis=1)
    s2 = jnp.concatenate([rs2_ref[...]] * reps, axis=1)
    half = GRID_W // 2

    def rope(z):
        return z * cos + pltpu.roll(z, width - half, axis=1) * s1 + pltpu.roll(z, half, axis=1) * s2

    q = jnp.dot(h, w_ref[:, n_rw:n_rw + width], preferred_element_type=F32)
    q_ref[0] = (rope(q) * scale).astype(BF16)
    k = jnp.dot(h, w_ref[:, n_rw + width:n_rw + 2 * width], preferred_element_type=F32)
    k_ref[0] = rope(k).astype(BF16)
    v = jnp.dot(h, w_ref[:, n_rw + 2 * width:n_rw + 3 * width], preferred_element_type=F32)
    v_ref[0] = v.astype(BF16)


def _inproj(x, ctx, mod1, g_pre, w_p, rope_c, rope_s1, rope_s2, n_rw, width, scale):
    b, t_lat, d = x.shape
    n_ctx = ctx.shape[1] // TOK_TILE
    n_tiles = t_lat // TOK_TILE + n_ctx
    t_all = n_tiles * TOK_TILE
    tok = lambda w: pl.BlockSpec((1, TOK_TILE, w), lambda i, t: (i, t, 0))
    return pl.pallas_call(
        functools.partial(_inproj_kernel, n_rw=n_rw, width=width, scale=scale),
        grid=(b, n_tiles),
        in_specs=[pl.BlockSpec((1, TOK_TILE, d), lambda i, t: (i, jnp.maximum(t - 1, 0), 0)),
                  pl.BlockSpec((1, TOK_TILE, d), lambda i, t: (i, 0, 0)),
                  pl.BlockSpec((1, 1, 2, d), lambda i, t: (i, jnp.minimum(t, 1), 0, 0)),
                  _const_spec((1, d)),
                  _const_spec(w_p.shape),
                  pl.BlockSpec((TOK_TILE, LANE), lambda i, t: (t, 0)),
                  pl.BlockSpec((TOK_TILE, LANE), lambda i, t: (t, 0)),
                  pl.BlockSpec((TOK_TILE, LANE), lambda i, t: (t, 0))],
        out_specs=[tok(n_rw),
                   pl.BlockSpec((1, TOK_TILE, width), lambda i, t: (i, jnp.maximum(t - n_ctx, 0), 0)),
                   tok(width), tok(width)],
        out_shape=[jax.ShapeDtypeStruct((b, t_all, n_rw), F32),
                   jax.ShapeDtypeStruct((b, t_lat, width), BF16),
                   jax.ShapeDtypeStruct((b, t_all, width), BF16),
                   jax.ShapeDtypeStruct((b, t_all, width), BF16)],
        compiler_params=_params("parallel", "arbitrary"),
        name="in_proj",
    )(x, ctx, mod1, g_pre, w_p, rope_c, rope_s1, rope_s2)


def _attn_kernel(q_ref, k_ref, v_ref, lam_ref, g_ref, o_ref, *, lam_init):
    lp = lam_ref[...]
    lam = (jnp.exp(jnp.sum(lp[0:1] * lp[1:2], axis=-1, keepdims=True))
           - jnp.exp(jnp.sum(lp[2:3] * lp[3:4], axis=-1, keepdims=True)) + lam_init)
    k = k_ref[0]
    d = k.shape[-1] // 2
    lane = lax.broadcasted_iota(jnp.int32, (ATTN_SUB, 2 * d), 1)
    zero = jnp.zeros((ATTN_SUB, 2 * d), BF16)

    scores = []
    for i in range(ATTN_TILE // ATTN_SUB):
        q = q_ref[0, i * ATTN_SUB:(i + 1) * ATTN_SUB, :]
        scores.append([lax.dot_general(jnp.where(sel, q, zero), k, _NT, preferred_element_type=F32)
                       for sel in (lane < d, lane >= d)])
    v_ones = jnp.concatenate([v_ref[0], jnp.ones_like(v_ref[0])], axis=1)
    for i, (s1, s2) in enumerate(scores):
        outs = []
        for s in (s1, s2):
            p = jnp.exp2(s - jnp.max(s, axis=-1, keepdims=True)).astype(BF16)
            ol = jnp.dot(p, v_ones, preferred_element_type=F32)
            outs.append(ol[:, :2 * d] / ol[:, 2 * d:2 * d + 1])
        o = outs[0] - lam * outs[1]
        o_ref[0, i * ATTN_SUB:(i + 1) * ATTN_SUB, :] = (
            _rms(o, g_ref[...], SUBLN_EPS) * (1.0 - lam_init)).astype(BF16)


def _attention(q, k, v, lam_p, subln_w, t_lat, lam_init):
    b, t_all, width = k.shape
    hd = subln_w.shape[-1]
    heads = width // hd
    return pl.pallas_call(
        functools.partial(_attn_kernel, lam_init=lam_init),
        grid=(b, heads, t_lat // ATTN_TILE),
        in_specs=[pl.BlockSpec((1, ATTN_TILE, hd), lambda i, h, j: (i, j, h)),
                  pl.BlockSpec((1, t_all, hd), lambda i, h, j: (i, 0, h)),
                  pl.BlockSpec((1, t_all, hd), lambda i, h, j: (i, 0, h)),
                  _const_spec(lam_p.shape),
                  _const_spec((1, hd))],
        out_specs=pl.BlockSpec((1, ATTN_TILE, hd), lambda i, h, j: (i, j, h)),
        out_shape=jax.ShapeDtypeStruct((b, t_lat, width), BF16),
        compiler_params=_params("parallel", "parallel", "arbitrary"),
        name="diff_attn",
    )(q, k, v, lam_p, subln_w)


def _build_kernel(main_ref, prev_ref, next_ref, cw_ref, kk_ref, ka_ref, rk_ref,
                  w0_ref, w2_ref, a0_ref, a2_ref, g2_ref,
                  rp_ref, bon_ref, gate_ref,
                  khz_s, rhz_s, vz_s, gamz_s, kt_s, bt_s, kb_s, bb_s, *, n_tiles, rw, heads):
    tile = pl.program_id(1)
    hd = rw // heads
    n_chunks = TOK_TILE // CHUNK

    group_chunks = BUILD_GROUP_CHUNKS
    group_rows = group_chunks * CHUNK
    n_groups = n_chunks // group_chunks
    starts = (tile == 0) | (tile == 1)
    ends = (tile == 0) | (tile == n_tiles - 1)
    cw = cw_ref[...]
    ri = lax.broadcasted_iota(jnp.int32, (group_rows, group_rows), 0)
    ci = lax.broadcasted_iota(jnp.int32, (group_rows, group_rows), 1)
    same = (ri // CHUNK) == (ci // CHUNK)

    def prepare(g):
        r0 = g * group_rows
        rows = slice(r0, r0 + group_rows)
        x = main_ref[0, rows, :]
        if g > 0:
            prev_row = main_ref[0, r0 - 1:r0, :]
        else:
            prev_row = jnp.where(starts, 0.0, prev_ref[0][SUBLANE - 1:SUBLANE, :])
        if g < n_groups - 1:
            next_row = main_ref[0, r0 + group_rows:r0 + group_rows + 1, :]
        else:
            next_row = jnp.where(ends, 0.0, next_ref[0][0:1, :])
        xm1, xp1 = _shift_rows(x, prev_row, next_row)
        xc = cw[0:1] * xm1 + cw[1:2] * x + cw[2:3] * xp1

        r = xc[:, 0:rw]
        k = xc[:, rw:2 * rw]
        v = xc[:, 2 * rw:3 * rw]
        wl = xc[:, 3 * rw:3 * rw + LANE]
        al = xc[:, 3 * rw + LANE:3 * rw + 2 * LANE]
        gl = xc[:, 3 * rw + 2 * LANE:3 * rw + 3 * LANE]
        gate_ref[0, rows, :] = _dot(_sigmoid(gl), g2_ref[...]).astype(BF16)

        kk = k * kk_ref[...]
        kk = kk / jnp.maximum(jnp.sqrt(_group_sum(kk * kk, hd)), 1e-12)
        w_lat = jnp.tanh(wl).astype(BF16)
        al_b = al.astype(BF16)
        bonus = jnp.zeros((group_rows, rw), F32)
        for d in range(2):
            z = w0_ref[d] + jnp.dot(w_lat, w2_ref[d], preferred_element_type=F32)
            ld = -math.exp(-0.5) * _sigmoid(z)
            a = _sigmoid(a0_ref[d] + jnp.dot(al_b, a2_ref[d], preferred_element_type=F32))
            kd = k * (1.0 + (a - 1.0) * ka_ref[...])
            bvec = kk * a
            bonus = bonus + 0.5 * _group_sum(r * kd * rk_ref[...], hd) * v

            before = (ci <= ri) if d == 0 else (ci >= ri)
            cum = _dot_lh(jnp.where(same & before, 1.0, 0.0).astype(BF16), ld)
            last = [c * CHUNK + CHUNK - 1 if d == 0 else c * CHUNK for c in range(group_chunks)]
            ctot = jnp.concatenate([jnp.broadcast_to(cum[i:i + 1], (CHUNK, rw)) for i in last], axis=0)
            e_pos = jnp.exp(cum)
            e_neg = jnp.exp(-cum)
            e_rem = jnp.exp(ctot - cum)
            gam = jnp.exp(jnp.concatenate([ctot[c * CHUNK:c * CHUNK + SUBLANE] for c in range(group_chunks)],
                                          axis=0))
            narrow = ((kt_s, kd * e_neg), (bt_s, bvec * e_neg), (kb_s, kd * e_rem), (bb_s, bvec * e_rem))
            wide_lo = ((khz_s, kk * jnp.exp(cum - ld)), (rhz_s, r * e_pos))
            g_rows = slice(g * group_chunks * SUBLANE, (g + 1) * group_chunks * SUBLANE)
            for h in range(heads):
                sl = slice(h * hd, (h + 1) * hd)
                for ref, val in narrow:
                    ref[d * heads + h, rows, :] = val[:, sl].astype(BF16)
                for ref, val in wide_lo:
                    lo = val[:, sl]
                    ref[d * heads + h, rows, :] = jnp.concatenate([lo, jnp.zeros_like(lo)], axis=1).astype(BF16)
                lo = gam[:, sl]
                gamz_s[d * heads + h, g_rows, :] = jnp.concatenate([lo, jnp.zeros_like(lo)], axis=1)
        bon_ref[0, rows, :] = bonus.astype(BF16)
        for h in range(heads):
            hi = v[:, h * hd:(h + 1) * hd]
            vz_s[h, rows, :] = jnp.concatenate([jnp.zeros_like(hi), hi], axis=1).astype(BF16)

    ii = lax.broadcasted_iota(jnp.int32, (CHUNK, CHUNK), 0)
    jj = lax.broadcasted_iota(jnp.int32, (CHUNK, CHUNK), 1)
    eye_f = jnp.where(ii == jj, 1.0, 0.0)
    level_masks = []
    size = 1
    while size < CHUNK:
        level_masks.append(((ii // (2 * size)) == (jj // (2 * size))) & ((ii // size) != (jj // size)))
        size *= 2
    iw = lax.broadcasted_iota(jnp.int32, (CHUNK, 2 * CHUNK), 0)
    jw = lax.broadcasted_iota(jnp.int32, (CHUNK, 2 * CHUNK), 1)
    jw = jnp.where(jw >= CHUNK, jw - CHUNK, jw)
    strict = (jw < iw, jw > iw)
    incl = (jw <= iw, jw >= iw)
    eye_wide = (lax.broadcasted_iota(jnp.int32, (hd, 2 * hd), 0)
                == lax.broadcasted_iota(jnp.int32, (hd, 2 * hd), 1))
    zero_top = jnp.zeros((CHUNK, 2 * hd), BF16)

    def build_group(chains):
        n = range(len(chains))
        rows = [slice(c * CHUNK, (c + 1) * CHUNK) for _, _, c in chains]
        ld_ = lambda ref: [ref[d * heads + h, rows[i], :] for i, (d, h, _) in enumerate(chains)]
        khz, rhz, kt, bt, kb, bb = (ld_(ref) for ref in (khz_s, rhz_s, kt_s, bt_s, kb_s, bb_s))
        vz = [vz_s[h, rows[i], :] for i, (_, h, _) in enumerate(chains)]
        g_row = [gamz_s[d * heads + h, c * SUBLANE:c * SUBLANE + 1, :] for d, h, c in chains]
        gram = [_dot_nt(jnp.concatenate([khz[i][:, :hd], rhz[i][:, :hd]], axis=0),
                        jnp.concatenate([bt[i], kt[i]], axis=0)) for i in n]
        mm = [jnp.where(strict[chains[i][0]], gram[i][:CHUNK], 0.0) for i in n]
        aa = [jnp.where(incl[chains[i][0]], gram[i][CHUNK:], 0.0) for i in n]
        m_ab = [mm[i][:, :CHUNK] for i in n]
        tinv = [eye_f - jnp.where(level_masks[0], m_ab[i], 0.0) for i in n]
        for lm in level_masks[1:]:
            right = [_dot(jnp.where(lm, m_ab[i], 0.0), tinv[i]) for i in n]
            tinv = [tinv[i] - _dot(tinv[i], right[i]) for i in n]
        zmat = [khz[i].astype(F32) + _dot(mm[i], jnp.concatenate([zero_top, vz[i]], axis=0)) for i in n]
        xmat = [_dot(tinv[i], zmat[i]) for i in n]
        rhs = [jnp.concatenate([(-xmat[i]).astype(BF16), vz[i]], axis=0) for i in n]
        for i, (d, h, c) in enumerate(chains):
            idx = (d * heads + h) * n_chunks + c
            ry = rhz[i].astype(F32) + _dot(aa[i], rhs[i])
            pq = jnp.where(eye_wide, g_row[i], 0.0) + _dot_tn(jnp.concatenate([bb[i], kb[i]], axis=0), rhs[i])
            rp_ref[0, 0, idx, 0:CHUNK, :] = ry.astype(BF16)
            rp_ref[0, 0, idx, CHUNK:CHUNK + hd, :] = pq.astype(BF16)

    def group_chains(g):
        return [(d, h, c) for d in range(2) for c in range(g * group_chunks, (g + 1) * group_chunks)
                for h in range(heads)]

    for g in range(n_groups):
        prepare(g)
        build_group(group_chains(g))


def _rwkv_build(rwp, cw, k_k, k_a, r_k, w0, w2p, a0, a2p, g2p, t_lat, rw, heads):
    b, t_all, n_rw = rwp.shape
    n_tiles = t_all // TOK_TILE
    n_ctx = n_tiles - t_lat // TOK_TILE
    hd = rw // heads
    halo = TOK_TILE // SUBLANE
    n_halo = t_all // SUBLANE
    n_chunks = TOK_TILE // CHUNK
    n_mat = 2 * heads * n_chunks
    narrow = pltpu.VMEM((2 * heads, TOK_TILE, hd), BF16)
    wide = pltpu.VMEM((2 * heads, TOK_TILE, 2 * hd), BF16)
    lat = pl.BlockSpec((1, TOK_TILE, rw), lambda i, t: (i, jnp.maximum(t - n_ctx, 0), 0))
    return pl.pallas_call(
        functools.partial(_build_kernel, n_tiles=n_tiles, rw=rw, heads=heads),
        grid=(b, n_tiles),
        in_specs=[pl.BlockSpec((1, TOK_TILE, n_rw), lambda i, t: (i, t, 0)),
                  pl.BlockSpec((1, SUBLANE, n_rw), lambda i, t: (i, jnp.maximum(t * halo - 1, 0), 0)),
                  pl.BlockSpec((1, SUBLANE, n_rw), lambda i, t: (i, jnp.minimum((t + 1) * halo, n_halo - 1), 0)),
                  _const_spec(cw.shape), _const_spec((1, rw)), _const_spec((1, rw)), _const_spec((1, rw)),
                  _const_spec(w0.shape), _const_spec(w2p.shape), _const_spec(a0.shape), _const_spec(a2p.shape),
                  _const_spec(g2p.shape)],
        out_specs=[pl.BlockSpec((1, 1, n_mat, CHUNK + hd, 2 * hd), lambda i, t: (i, t, 0, 0, 0)), lat, lat],
        out_shape=[jax.ShapeDtypeStruct((b, n_tiles, n_mat, CHUNK + hd, 2 * hd), BF16),
                   jax.ShapeDtypeStruct((b, t_lat, rw), BF16),
                   jax.ShapeDtypeStruct((b, t_lat, rw), BF16)],
        scratch_shapes=[wide, wide, pltpu.VMEM((heads, TOK_TILE, 2 * hd), BF16),
                        pltpu.VMEM((2 * heads, n_chunks * SUBLANE, 2 * hd), F32),
                        narrow, narrow, narrow, narrow],
        compiler_params=_params("parallel", "arbitrary"),
        name="rwkv_build",
    )(rwp, rwp, rwp, cw, k_k, k_a, r_k, w0, w2p, a0, a2p, g2p)


def _chain_kernel(rpf_ref, rpb_ref, yf_ref, yb_ref, h_s, *, heads, hd):
    n_chunks = TOK_TILE // CHUNK

    @pl.when(pl.program_id(1) == 0)
    def _():
        h_s[...] = jnp.zeros_like(h_s)

    ii = lax.broadcasted_iota(jnp.int32, (hd, hd), 0)
    jj = lax.broadcasted_iota(jnp.int32, (hd, hd), 1)
    eye_b = jnp.where(ii == jj, 1.0, 0.0).astype(BF16)
    state = [h_s[i] for i in range(2 * heads)]
    for j in range(n_chunks):
        jobs = [(0, rpf_ref, yf_ref, j), (1, rpb_ref, yb_ref, n_chunks - 1 - j)]
        out = [[jnp.dot(ref[0, 0, h * n_chunks + c],
                        jnp.concatenate([state[d * heads + h].astype(BF16), eye_b], axis=0),
                        preferred_element_type=F32) for h in range(heads)] for d, ref, _, c in jobs]
        for d, _, y_ref, c in jobs:
            for h in range(heads):
                state[d * heads + h] = out[d][h][CHUNK:]
                y_ref[0, c * CHUNK:(c + 1) * CHUNK, h * hd:(h + 1) * hd] = out[d][h][:CHUNK].astype(BF16)
    for i in range(2 * heads):
        h_s[i] = state[i]


def _rwkv_chain(rp, t_lat, rw, heads):
    b, n_tiles, n_mat = rp.shape[:3]
    n_lat = t_lat // TOK_TILE
    hd = rw // heads
    per_dir = n_mat // 2

    def bwd_tile(s):
        return jnp.where(s == 0, 0, n_tiles - s)

    mats = lambda d, tile: pl.BlockSpec((1, 1, per_dir, CHUNK + hd, 2 * hd),
                                        lambda i, s: (i, tile(s), d, 0, 0))
    yf_spec = pl.BlockSpec((1, TOK_TILE, rw), lambda i, s: (i, jnp.maximum(s - (n_tiles - n_lat), 0), 0))
    yb_spec = pl.BlockSpec((1, TOK_TILE, rw),
                           lambda i, s: (i, jnp.where(s == 0, n_lat - 1, bwd_tile(s) - (n_tiles - n_lat)), 0))
    return pl.pallas_call(
        functools.partial(_chain_kernel, heads=heads, hd=hd),
        grid=(b, n_tiles),
        in_specs=[mats(0, lambda s: s), mats(1, bwd_tile)],
        out_specs=[yf_spec, yb_spec],
        out_shape=[jax.ShapeDtypeStruct((b, t_lat, rw), BF16), jax.ShapeDtypeStruct((b, t_lat, rw), BF16)],
        scratch_shapes=[pltpu.VMEM((2 * heads, hd, hd), F32)],
        compiler_params=_params("parallel", "arbitrary"),
        name="rwkv_chain",
    )(rp, rp)


def _mix_kernel(yf_ref, yb_ref, bon_ref, gate_ref, lnw_ref, lnb_ref, yd_ref, wo_ref, gpost_ref,
                x_ref, gt_ref, gpre_ref, mod_ref, x1_ref, h2_ref, *, rw, heads):
    hd = rw // heads
    y = yf_ref[0].astype(F32) + yb_ref[0].astype(F32)
    mu = _group_sum(y, hd, exact=True) * (1.0 / hd)
    yc = y - mu
    var = _group_sum(yc * yc, hd, exact=True) * (1.0 / hd)
    yn = yc * lax.rsqrt(var + LN_X_EPS) * lnw_ref[...] + lnb_ref[...]
    yr = ((yn + bon_ref[0].astype(F32)) * gate_ref[0].astype(F32)).astype(BF16)
    m = (jnp.dot(yr, wo_ref[0:rw, :], preferred_element_type=F32)
         + jnp.dot(yd_ref[0], wo_ref[rw:, :], preferred_element_type=F32))
    x1 = x_ref[0] + gt_ref[0] * _rms(m, gpost_ref[...], NORM_EPS)
    x1_ref[0] = x1
    mod = mod_ref[0]
    h2_ref[0] = (_rms(x1, gpre_ref[...], NORM_EPS) * (1.0 + mod[1:2]) + mod[0:1]).astype(BF16)


def _mix_out(yf, yb, bon, gate, ln_w, ln_b, yd, w_out, g_post, x, gt1, g_pre2, mod2, rw, heads):
    b, t_lat, d = x.shape
    tok = lambda w: pl.BlockSpec((1, MIX_TILE, w), lambda i, j: (i, j, 0))
    return pl.pallas_call(
        functools.partial(_mix_kernel, rw=rw, heads=heads),
        grid=(b, t_lat // MIX_TILE),
        in_specs=[tok(rw), tok(rw), tok(rw), tok(rw),
                  _const_spec((1, rw)), _const_spec((1, rw)),
                  tok(d - rw), _const_spec((d, d)), _const_spec((1, d)),
                  tok(d),
                  pl.BlockSpec((1, 1, d), lambda i, j: (i, 0, 0)),
                  _const_spec((1, d)),
                  pl.BlockSpec((1, 2, d), lambda i, j: (i, 0, 0))],
        out_specs=[tok(d), tok(d)],
        out_shape=[jax.ShapeDtypeStruct((b, t_lat, d), F32),
                   jax.ShapeDtypeStruct((b, t_lat, d), BF16)],
        compiler_params=_params("parallel", "arbitrary"),
        name="mix_out",
    )(yf, yb, bon, gate, ln_w, ln_b, yd, w_out, g_post, x, gt1, g_pre2, mod2)


def _ffn_kernel(h_ref, hp_ref, hn_ref, wu_ref, cw_ref, cb_ref, wd_ref, x1_ref, gt_ref, gpost_ref,
                o_ref, act_s, *, n_tiles, d_ff):
    j = pl.program_id(1)
    first = j == 0
    last = j == n_tiles - 1
    lhs = jnp.concatenate([h_ref[0], hp_ref[0], hn_ref[0]], axis=0)
    prev_at = FFN_TILE + FFN_HALO - 1
    next_at = FFN_TILE + FFN_HALO
    starts = list(range(0, d_ff, FFN_COLS))

    def up(c0):
        return [jnp.dot(lhs, wu_ref[:, base:base + FFN_COLS], preferred_element_type=F32)
                for base in (c0, d_ff + c0)]

    def conv(u, base):
        cols = slice(base, base + FFN_COLS)
        main = u[:FFN_TILE]
        um1, up1 = _shift_rows(main, jnp.where(first, 0.0, u[prev_at:prev_at + 1]),
                               jnp.where(last, 0.0, u[next_at:next_at + 1]))
        cw = cw_ref[:, cols]
        return cw[0:1] * um1 + cw[1:2] * main + cw[2:3] * up1 + cb_ref[:, cols]

    acc = None
    raw = up(starts[0])
    for i, c0 in enumerate(starts):
        nxt = up(starts[i + 1]) if i + 1 < len(starts) else None
        val = conv(raw[0], c0)
        gate = conv(raw[1], d_ff + c0)
        act_s[:, c0:c0 + FFN_COLS] = (val * gate * _sigmoid(gate)).astype(BF16)
        raw = nxt
        if (i + 1) % FFN_DOWN_GROUP == 0 or i + 1 == len(starts):
            k0 = (i // FFN_DOWN_GROUP) * FFN_DOWN_GROUP * FFN_COLS
            part = jnp.dot(act_s[:, k0:c0 + FFN_COLS], wd_ref[k0:c0 + FFN_COLS, :], preferred_element_type=F32)
            acc = part if acc is None else acc + part
    o_ref[0] = x1_ref[0] + gt_ref[0] * _rms(acc, gpost_ref[...], NORM_EPS)


def _ffn(h2, w_up, conv_w, conv_b, w_down, x1, gt2, g_post):
    b, t_lat, d = x1.shape
    d_ff = w_down.shape[0]
    n_tiles = t_lat // FFN_TILE
    halo = FFN_TILE // FFN_HALO
    n_halo = t_lat // FFN_HALO
    tok = pl.BlockSpec((1, FFN_TILE, d), lambda i, j: (i, j, 0))
    return pl.pallas_call(
        functools.partial(_ffn_kernel, n_tiles=n_tiles, d_ff=d_ff),
        grid=(b, n_tiles),
        in_specs=[tok,
                  pl.BlockSpec((1, FFN_HALO, d), lambda i, j: (i, jnp.maximum(j * halo - 1, 0), 0)),
                  pl.BlockSpec((1, FFN_HALO, d), lambda i, j: (i, jnp.minimum((j + 1) * halo, n_halo - 1), 0)),
                  _const_spec(w_up.shape), _const_spec(conv_w.shape), _const_spec(conv_b.shape),
                  _const_spec(w_down.shape),
                  tok,
                  pl.BlockSpec((1, 1, d), lambda i, j: (i, 0, 0)),
                  _const_spec((1, d))],
        out_specs=tok,
        out_shape=jax.ShapeDtypeStruct((b, t_lat, d), F32),
        scratch_shapes=[pltpu.VMEM((FFN_TILE, d_ff), BF16)],
        compiler_params=_params("parallel", "arbitrary"),
        name="conv_ffn",
    )(h2, h2, h2, w_up, conv_w, conv_b, w_down, x1, gt2, g_post)


def _rope_tables(t_ctx, t_lat, d_head):
    n_pair = d_head // 4
    pos = jnp.arange(t_lat, dtype=F32)
    row = jnp.floor(pos / GRID_W)
    col = pos - row * GRID_W
    inv = ROPE_THETA ** (-jnp.arange(n_pair, dtype=F32) / n_pair)
    ang = jnp.concatenate([row[:, None] * inv, col[:, None] * inv], axis=-1)
    ang = jnp.concatenate([jnp.zeros((t_ctx, d_head // 2), F32), ang], axis=0)
    cos, sin = jnp.cos(ang), jnp.sin(ang)
    zero = jnp.zeros_like(sin)
    reps = LANE // d_head
    tile = lambda a, b_: jnp.tile(jnp.concatenate([a, b_], axis=-1), (1, reps))
    return tile(cos, cos), tile(-sin, zero), tile(zero, sin)


def _pad_cols(a, width):
    return jnp.pad(a, ((0, 0), (0, width - a.shape[1])))


def _pad_rows(a, height):
    return jnp.pad(a, ((0, height - a.shape[0]), (0, 0)))


def kernel(x, c, ctx, c_ctx, w_mod, b_mod, g_pre_mix, g_post_mix, g_pre_ffn, g_post_ffn, w_in, rwkv_conv, w0_fwd, w2_fwd, a0_fwd, a2_fwd, w0_bwd, w2_bwd, a0_bwd, a2_bwd, g2, k_k, k_a, r_k, ln_x_w, ln_x_b, lam_q1, lam_k1, lam_q2, lam_k2, subln_w, w_out, w_up, ffn_conv, ffn_conv_b, w_down):
    depth = w_in.shape[0]
    assert depth == 1, "single-layer block only"
    b, t_lat, d = x.shape
    t_ctx = ctx.shape[1]
    assert t_ctx == TOK_TILE and t_lat % FFN_TILE == 0 and t_lat % GRID_W == 0
    rw = k_k.shape[-1]
    heads = r_k.shape[1]
    d_head = lam_q1.shape[-1]
    width = d - rw
    n_lora = (w2_fwd.shape[1], a2_fwd.shape[1], g2.shape[1])
    assert max(n_lora) <= LANE
    lam_init = 0.8 - 0.6 * math.exp(-0.3 * 0)

    cc = _pad_rows(jnp.concatenate([c, c_ctx[None, :]], axis=0), 2 * SUBLANE)
    mod = _modulation(cc, w_mod[0], b_mod)
    sh1, sc1, gt1, sh2, sc2, gt2 = jnp.split(mod[:b], 6, axis=-1)
    sh1c, sc1c = mod[b, 0:d], mod[b, d:2 * d]
    mod1 = jnp.stack([jnp.stack([jnp.broadcast_to(sh1c, (b, d)), jnp.broadcast_to(sc1c, (b, d))], axis=1),
                      jnp.stack([sh1, sc1], axis=1)], axis=1)
    mod2 = jnp.stack([sh2, sc2], axis=1)

    w = w_in[0]
    o = 3 * rw
    cuts = (o, o + n_lora[0], o + n_lora[0] + n_lora[1], o + sum(n_lora))
    n_rw = o + 3 * LANE
    w_p = jnp.concatenate([w[:, :o], _pad_cols(w[:, cuts[0]:cuts[1]], LANE), _pad_cols(w[:, cuts[1]:cuts[2]], LANE),
                           _pad_cols(w[:, cuts[2]:cuts[3]], LANE), w[:, cuts[3]:]], axis=1).astype(BF16)
    cv = rwkv_conv[0]
    cw = jnp.concatenate([cv[:, :o], _pad_cols(cv[:, cuts[0]:cuts[1]], LANE), _pad_cols(cv[:, cuts[1]:cuts[2]], LANE),
                          _pad_cols(cv[:, cuts[2]:cuts[3]], LANE)], axis=1)
    rope_c, rope_s1, rope_s2 = _rope_tables(t_ctx, t_lat, d_head)

    rwp, q, k, v = _inproj(x, ctx, mod1, g_pre_mix, w_p, rope_c, rope_s1, rope_s2, n_rw, width,
                           float(d_head) ** -0.5 * math.log2(math.e))

    lam_p = jnp.concatenate([lam_q1, lam_k1, lam_q2, lam_k2], axis=0)
    yd = _attention(q, k, v, lam_p, subln_w, t_lat, lam_init)

    w0 = jnp.concatenate([w0_fwd, w0_bwd], axis=0)[:, None, :]
    a0 = jnp.concatenate([a0_fwd, a0_bwd], axis=0)[:, None, :]
    w2p = jnp.stack([_pad_rows(w2_fwd[0], LANE), _pad_rows(w2_bwd[0], LANE)], axis=0).astype(BF16)
    a2p = jnp.stack([_pad_rows(a2_fwd[0], LANE), _pad_rows(a2_bwd[0], LANE)], axis=0).astype(BF16)
    g2p = _pad_rows(g2[0], LANE).astype(BF16)
    rp, bon, gate = _rwkv_build(rwp, cw, k_k, k_a, r_k.reshape(1, rw), w0, w2p, a0, a2p, g2p, t_lat, rw, heads)
    yf, yb = _rwkv_chain(rp, t_lat, rw, heads)

    x1, h2 = _mix_out(yf, yb, bon, gate, ln_x_w, ln_x_b, yd, w_out[0].astype(BF16), g_post_mix, x,
                      gt1[:, None, :], g_pre_ffn, mod2, rw, heads)

    return _ffn(h2, w_up[0].astype(BF16), ffn_conv[0], ffn_conv_b, w_down[0].astype(BF16), x1,
                gt2[:, None, :], g_post_ffn)
```
